```python
import jax, jax.numpy as jnp
from jax import lax
import numpy as np

D_MODEL = 1024
BATCH = 16
SEQ = 256
DEPTH = 1
DEC_BATCH = 4
DEC_SEQ = 4096
PAST_LEN = 512

GRID_W = 64
N_HEADS = 8
N_KV = 2
GROUP = N_HEADS // N_KV
HEAD_DIM = 128
R_HEADS = 8
R_DK = 64
R_DV = 128
D_FF = 2816
CONV_W = 3
CHUNK = 128
Q_BLOCK = 128
ROPE_THETA = 10000.0
EPS = 1e-6

ATT_Q = N_HEADS * HEAD_DIM
ATT_KV = N_KV * HEAD_DIM
RET_QK = R_HEADS * R_DK
RET_V = R_HEADS * R_DV
IN_SPLITS = (ATT_Q, ATT_KV, ATT_KV, RET_QK, RET_QK, RET_V, RET_V, D_MODEL, D_MODEL)
D_IN = ATT_Q + 2 * ATT_KV + 2 * RET_QK + 2 * RET_V + 2 * D_MODEL

kernel_name = "hybrid_retention_gqa_dit_step"


def rms_scale(x):
    xf = x.astype(jnp.float32)
    return xf * lax.rsqrt(jnp.mean(xf * xf, axis=-1, keepdims=True) + EPS)


def rmsnorm(x, g):
    return (rms_scale(x) * g.astype(jnp.float32)).astype(x.dtype)


def rope_1d(x, pos):
    half = x.shape[-1] // 2
    inv = ROPE_THETA ** (-jnp.arange(half, dtype=jnp.float32) / half)
    ang = pos.astype(jnp.float32)[:, None] * inv[None, :]
    cos, sin = jnp.cos(ang), jnp.sin(ang)
    x1 = x[..., :half].astype(jnp.float32)
    x2 = x[..., half:].astype(jnp.float32)
    return jnp.concatenate([x1 * cos - x2 * sin, x2 * cos + x1 * sin], axis=-1).astype(x.dtype)


def rope_2d(x):
    L = x.shape[-2]
    rows = L // GRID_W
    t = jnp.arange(rows * GRID_W)
    half = x.shape[-1] // 2
    return jnp.concatenate([rope_1d(x[..., :half], t // GRID_W), rope_1d(x[..., half:], t % GRID_W)], axis=-1)


def adaln(cond, w_mod, b_mod):
    m = jax.nn.silu(cond) @ w_mod + b_mod
    return jnp.split(m[..., None, :], 6, axis=-1)


def attend_blocks(q, k, v):
    B, KV, G, Lq, hd = q.shape
    nb = Lq // Q_BLOCK
    qb = q.reshape(B, KV, G, nb, Q_BLOCK, hd).transpose(3, 0, 1, 2, 4, 5)
    scale = hd ** -0.5

    def one_block(qblk):
        s = jnp.einsum("bkgqd,bksd->bkgqs", qblk, k, preferred_element_type=jnp.float32) * scale
        p = jax.nn.softmax(s, axis=-1)
        return jnp.einsum("bkgqs,bksd->bkgqd", p.astype(v.dtype), v)

    o = lax.map(one_block, qb)
    o = o.transpose(1, 2, 3, 0, 4, 5).reshape(B, KV * G, Lq, hd)
    return o.transpose(0, 2, 1, 3).reshape(B, Lq, KV * G * hd)


def retention_scan(q, k, v, log_gamma, s0):
    B, H, L, dk = q.shape
    dv = v.shape[-1]
    n = L // CHUNK
    qc = q.reshape(B, H, n, CHUNK, dk)
    kc = k.reshape(B, H, n, CHUNK, dk)
    vc = v.reshape(B, H, n, CHUNK, dv)
    i = jnp.arange(CHUNK, dtype=jnp.float32)
    lg = log_gamma[:, None]
    diff = i[:, None] - i[None, :]
    intra_decay = jnp.where(diff >= 0, jnp.exp(lg[:, :, None] * jnp.maximum(diff, 0.0)), 0.0)
    q_decay = jnp.exp(lg * (i + 1.0))
    k_decay = jnp.exp(lg * (CHUNK - 1.0 - i))
    chunk_decay = jnp.exp(log_gamma * CHUNK)[None, :, None, None]
    scores = jnp.einsum("bhnid,bhnjd->bhnij", qc, kc, preferred_element_type=jnp.float32)
    scores = scores * intra_decay[None, :, None]
    intra = jnp.einsum("bhnij,bhnjv->bhniv", scores, vc.astype(jnp.float32))
    kv = jnp.einsum("bhnjd,bhnjv->nbhdv", kc.astype(jnp.float32) * k_decay[None, :, None, :, None],
                    vc.astype(jnp.float32))

    def step(s, kv_n):
        return chunk_decay * s + kv_n, s

    s_final, s_prev = lax.scan(step, s0.astype(jnp.float32), kv)
    cross = jnp.einsum("bhnid,nbhdv->bhniv", qc.astype(jnp.float32) * q_decay[None, :, None, :, None], s_prev)
    return (intra + cross).reshape(B, H, L, dv), s_final


def retention_branch(rq, rk, rv, rg, pos, dec_f, dec_b, s0_f, s0_b):
    B, L, _ = rq.shape
    q = rope_1d(rq.reshape(B, L, R_HEADS, R_DK).transpose(0, 2, 1, 3), pos)
    k = rope_1d(rk.reshape(B, L, R_HEADS, R_DK).transpose(0, 2, 1, 3), pos) * (R_DK ** -0.5)
    v = rv.reshape(B, L, R_HEADS, R_DV).transpose(0, 2, 1, 3)
    lg_f = jax.nn.log_sigmoid(dec_f.astype(jnp.float32))
    lg_b = jax.nn.log_sigmoid(dec_b.astype(jnp.float32))
    o_f, s_f = retention_scan(q, k, v, lg_f, s0_f)
    o_b, s_b = retention_scan(q[:, :, ::-1], k[:, :, ::-1], v[:, :, ::-1], lg_b, s0_b)
    o = rms_scale(o_f + o_b[:, :, ::-1])
    o = o.transpose(0, 2, 1, 3).reshape(B, L, RET_V)
    return jax.nn.silu(rg.astype(jnp.float32)) * o, s_f, s_b


def token_mixer(h, w_in, q_g, k_g, dec_f, dec_b, w_att_o, w_ret_o, w_out, ret_pos, latent_ctx):
    B, L, _ = h.shape
    z = h @ w_in
    q, k, v, rq, rk, rv, rg, ga, gr = jnp.split(z, np.cumsum(IN_SPLITS)[:-1].tolist(), axis=-1)
    q = rmsnorm(q.reshape(B, L, N_KV, GROUP, HEAD_DIM), q_g).transpose(0, 2, 3, 1, 4)
    k = rmsnorm(k.reshape(B, L, N_KV, HEAD_DIM), k_g).transpose(0, 2, 1, 3)
    v = v.reshape(B, L, N_KV, HEAD_DIM).transpose(0, 2, 1, 3)
    if latent_ctx is None:
        k_all, v_all = k, v
        s0_f = jnp.zeros((B, R_HEADS, R_DK, R_DV), jnp.float32)
        s0_b = s0_f
    else:
        ck, cv, s0_f, s0_b = latent_ctx
        q = rope_2d(q)
        k = rope_2d(k)
        k_all = jnp.concatenate([k, ck.astype(k.dtype)], axis=2)
        v_all = jnp.concatenate([v, cv.astype(v.dtype)], axis=2)
    ya = attend_blocks(q, k_all, v_all)
    yr, s_f, s_b = retention_branch(rq, rk, rv, rg, ret_pos, dec_f, dec_b, s0_f, s0_b)
    y = jax.nn.sigmoid(ga) * (ya @ w_att_o) + jax.nn.sigmoid(gr) * (yr.astype(h.dtype) @ w_ret_o)
    return y @ w_out, (k, v, s_f, s_b)


def conv_ffn(h, w_up, conv_w, conv_b, w_down):
    L = h.shape[1]
    u = h @ w_up
    up = jnp.pad(u, ((0, 0), (1, 1), (0, 0)))
    uc = conv_w[0] * up[:, :L] + conv_w[1] * up[:, 1:L + 1] + conv_w[2] * up[:, 2:L + 2] + conv_b
    a, g = jnp.split(uc, 2, axis=-1)
    return (jax.nn.silu(g) * a) @ w_down


def setup_inputs(seed: int = 0) -> dict:
    key = jax.random.key(seed)
    ks = jax.random.split(key, 26)

    def nrm(k, shape, s):
        return jax.random.normal(k, shape, jnp.float32) * s

    base_decay = jnp.log(2.0 ** (5.0 + jnp.arange(R_HEADS, dtype=jnp.float32)) - 1.0)
    return {
        "x_prompt": nrm(ks[0], (BATCH, SEQ, D_MODEL), 1.0),
        "x_sample": nrm(ks[1], (DEC_BATCH, DEC_SEQ, D_MODEL), 1.0),
        "c": nrm(ks[2], (DEC_BATCH, D_MODEL), 1.0),
        "cache_k": nrm(ks[3], (DEC_BATCH, DEPTH, N_KV, PAST_LEN, HEAD_DIM), 1.0),
        "cache_v": nrm(ks[4], (DEC_BATCH, DEPTH, N_KV, PAST_LEN, HEAD_DIM), 1.0),
        "state_ret_fwd": nrm(ks[5], (DEC_BATCH, DEPTH, R_HEADS, R_DK, R_DV), 1.0),
        "state_ret_bwd": nrm(ks[6], (DEC_BATCH, DEPTH, R_HEADS, R_DK, R_DV), 1.0),
        "c_ctx": nrm(ks[7], (D_MODEL,), 1.0),
        "norm1_g": 1.0 + nrm(ks[8], (DEPTH, D_MODEL), 0.05),
        "norm2_g": 1.0 + nrm(ks[9], (DEPTH, D_MODEL), 0.05),
        "w_mod": nrm(ks[10], (DEPTH, D_MODEL, 6 * D_MODEL), D_MODEL ** -0.5),
        "b_mod": nrm(ks[11], (DEPTH, 6 * D_MODEL), 0.02),
        "w_in": nrm(ks[12], (DEPTH, D_MODEL, D_IN), D_MODEL ** -0.5),
        "q_norm_g": 1.0 + nrm(ks[13], (DEPTH, HEAD_DIM), 0.05),
        "k_norm_g": 1.0 + nrm(ks[14], (DEPTH, HEAD_DIM), 0.05),
        "decay_fwd": base_decay + nrm(ks[15], (DEPTH, R_HEADS), 0.1),
        "decay_bwd": base_decay + nrm(ks[16], (DEPTH, R_HEADS), 0.1),
        "w_att_o": nrm(ks[17], (DEPTH, ATT_Q, D_MODEL), ATT_Q ** -0.5),
        "w_ret_o": nrm(ks[18], (DEPTH, RET_V, D_MODEL), RET_V ** -0.5),
        "w_out": nrm(ks[19], (DEPTH, D_MODEL, D_MODEL), D_MODEL ** -0.5),
        "w_up": nrm(ks[20], (DEPTH, D_MODEL, 2 * D_FF), D_MODEL ** -0.5),
        "conv_w": nrm(ks[21], (DEPTH, CONV_W, 2 * D_FF), CONV_W ** -0.5),
        "conv_b": nrm(ks[22], (DEPTH, 2 * D_FF), 0.02),
        "w_down": nrm(ks[23], (DEPTH, D_FF, D_MODEL), D_FF ** -0.5),
        "final_g": 1.0 + nrm(ks[24], (D_MODEL,), 0.05),
    }


def reference(x_prompt, x_sample, c, cache_k, cache_v, state_ret_fwd, state_ret_bwd, c_ctx,
              norm1_g, norm2_g, w_mod, b_mod, w_in, q_norm_g, k_norm_g, decay_fwd, decay_bwd,
              w_att_o, w_ret_o, w_out, w_up, conv_w, conv_b, w_down, final_g):
    ctx_p = x_prompt.shape[1]
    ctx_s = cache_k.shape[3]
    lat = x_sample.shape[1]
    pos_ctx = jnp.arange(ctx_p)
    pos_lat = ctx_s + jnp.arange(lat)
    xp, xs = x_prompt, x_sample
    nk, nv, nsf, nsb = [], [], [], []
    for l in range(DEPTH):
        mix_w = (w_in[l], q_norm_g[l], k_norm_g[l], decay_fwd[l], decay_bwd[l], w_att_o[l], w_ret_o[l], w_out[l])
        sh1, sc1, g1, sh2, sc2, g2 = adaln(c_ctx, w_mod[l], b_mod[l])
        h = rmsnorm(xp, norm1_g[l]) * (1.0 + sc1) + sh1
        y, (k_c, v_c, sf_c, sb_c) = token_mixer(h, *mix_w, pos_ctx, None)
        xp = xp + g1 * y
        h = rmsnorm(xp, norm2_g[l]) * (1.0 + sc2) + sh2
        xp = xp + g2 * conv_ffn(h, w_up[l], conv_w[l], conv_b[l], w_down[l])
        nk.append(k_c)
        nv.append(v_c)
        nsf.append(sf_c)
        nsb.append(sb_c)
        sh1, sc1, g1, sh2, sc2, g2 = adaln(c, w_mod[l], b_mod[l])
        h = rmsnorm(xs, norm1_g[l]) * (1.0 + sc1) + sh1
        ctx = (cache_k[:, l], cache_v[:, l], state_ret_fwd[:, l], state_ret_bwd[:, l])
        y, _ = token_mixer(h, *mix_w, pos_lat, ctx)
        xs = xs + g1 * y
        h = rmsnorm(xs, norm2_g[l]) * (1.0 + sc2) + sh2
        xs = xs + g2 * conv_ffn(h, w_up[l], conv_w[l], conv_b[l], w_down[l])
    y_prompt = rmsnorm(xp, final_g)
    y_sample = rmsnorm(xs, final_g)
    new_cache_k = jnp.stack(nk, axis=1)
    new_cache_v = jnp.stack(nv, axis=1)
    new_state_ret_fwd = jnp.stack(nsf, axis=1)
    new_state_ret_bwd = jnp.stack(nsb, axis=1)
    return (y_prompt, y_sample, new_cache_k, new_cache_v, new_state_ret_fwd, new_state_ret_bwd)
```

```python
import functools

import jax
import jax.numpy as jnp
import numpy as np
from jax import lax
from jax.experimental import pallas as pl
from jax.experimental.pallas import tpu as pltpu

GRID_W = 64
ROPE_THETA = 10000.0
EPS = 1e-6
RET_CHUNK = 256
ROW_TILE = 512
Q_TILE = 256
KV_TILE = 512
HALO = 16
VMEM_LIMIT = 56 * 1024 * 1024

F32 = jnp.float32
BF16 = jnp.bfloat16


def _params(n_axes):
    return pltpu.CompilerParams(dimension_semantics=("arbitrary",) * n_axes,
                                vmem_limit_bytes=VMEM_LIMIT)


def _resident(shape):
    nd = len(shape)
    return pl.BlockSpec(shape, lambda *_: (0,) * nd, pipeline_mode=pl.Buffered(1))


def _rms(x):
    return x * lax.rsqrt(jnp.mean(x * x, axis=-1, keepdims=True) + EPS)


def _rotate(t, cos, sin_lo, sin_hi):
    return t * cos + pltpu.roll(t, 96, 1) * sin_lo + pltpu.roll(t, 32, 1) * sin_hi


def _rope_tables(pos, width):
    half = 32
    inv = ROPE_THETA ** (-jnp.arange(half, dtype=F32) / half)
    ang = pos.astype(F32)[:, None] * inv[None, :]
    cos, sin = jnp.cos(ang), jnp.sin(ang)
    zero = jnp.zeros_like(sin)
    reps = width // 64
    cos_t = jnp.tile(jnp.concatenate([cos, cos], axis=1), (1, reps))
    sin_lo = jnp.tile(jnp.concatenate([-sin, zero], axis=1), (1, reps))
    sin_hi = jnp.tile(jnp.concatenate([zero, sin], axis=1), (1, reps))
    return cos_t, sin_lo, sin_hi


def _mod_kernel(cond_ref, w_ref, b_ref, o_ref):
    cnd = cond_ref[...]
    s = (cnd * jax.nn.sigmoid(cnd)).astype(BF16)
    o_ref[...] = jnp.dot(s, w_ref[...].astype(BF16), preferred_element_type=F32) + b_ref[...]


def _modulation(cond, w_mod, b_mod):
    rows, d = cond.shape
    n = w_mod.shape[1]
    tn = 1024
    return pl.pallas_call(
        _mod_kernel,
        grid=(n // tn,),
        in_specs=[pl.BlockSpec((rows, d), lambda j: (0, 0)),
                  pl.BlockSpec((d, tn), lambda j: (0, j)),
                  pl.BlockSpec((1, tn), lambda j: (0, j))],
        out_specs=pl.BlockSpec((rows, tn), lambda j: (0, j)),
        out_shape=jax.ShapeDtypeStruct((rows, n), F32),
        compiler_params=_params(1),
        name="modulation",
    )(cond, w_mod, b_mod.reshape(1, n))


def _in_proj_kernel(*refs, dims, rope_qk, emit_cache, seq_len):
    n_heads, n_kv, hd, r_heads, r_dk, r_dv, d_model = dims
    it = iter(refs)
    x_ref, sh_ref, sc_ref, g_ref, w_ref, qg_ref, kg_ref = (next(it) for _ in range(7))
    rcos_ref, rlo_ref, rhi_ref = (next(it) for _ in range(3))
    if rope_qk:
        acos_ref, alo_ref, ahi_ref = (next(it) for _ in range(3))
    q_ref, k_ref, v_ref, rq_ref, rk_ref, rv_ref, rg_ref, ga_ref, gr_ref = (next(it) for _ in range(9))
    if emit_cache:
        ck_ref, cv_ref = next(it), next(it)

    x = x_ref[...]
    h = _rms(x) * g_ref[...]
    h = (h * (1.0 + sc_ref[0]) + sh_ref[0]).astype(BF16)

    col = [0]

    def proj(width):
        lo = col[0]
        col[0] = lo + width
        return jnp.dot(h, w_ref[:, lo:lo + width], preferred_element_type=F32)

    tm = x.shape[0]
    zq = proj(n_heads * hd)
    for i in range(n_heads):
        t = _rms(zq[:, i * hd:(i + 1) * hd]) * qg_ref[...]
        if rope_qk:
            t = _rotate(t, acos_ref[...], alo_ref[...], ahi_ref[...])
        q_ref[:, i * hd:(i + 1) * hd] = (t * (hd ** -0.5)).astype(BF16)
    zk = proj(n_kv * hd)
    zv = proj(n_kv * hd)
    v_ref[...] = zv.astype(BF16)
    for i in range(n_kv):
        t = _rms(zk[:, i * hd:(i + 1) * hd]) * kg_ref[...]
        if emit_cache:
            for s in range(tm // seq_len):
                ck_ref[s, 0, i] = t[s * seq_len:(s + 1) * seq_len]
                cv_ref[s, 0, i] = zv[s * seq_len:(s + 1) * seq_len, i * hd:(i + 1) * hd]
        if rope_qk:
            t = _rotate(t, acos_ref[...], alo_ref[...], ahi_ref[...])
        k_ref[:, i * hd:(i + 1) * hd] = t.astype(BF16)
    zrq = proj(r_heads * r_dk)
    zrk = proj(r_heads * r_dk)
    for i in range(r_heads * r_dk // 128):
        sl = slice(i * 128, (i + 1) * 128)
        rq_ref[:, sl] = _rotate(zrq[:, sl], rcos_ref[...], rlo_ref[...], rhi_ref[...]).astype(BF16)
        t = _rotate(zrk[:, sl], rcos_ref[...], rlo_ref[...], rhi_ref[...])
        rk_ref[:, sl] = (t * (r_dk ** -0.5)).astype(BF16)
    rv_ref[...] = proj(r_heads * r_dv).astype(BF16)
    zg = proj(r_heads * r_dv)
    rg_ref[...] = (zg * jax.nn.sigmoid(zg)).astype(BF16)
    ga_ref[...] = jax.nn.sigmoid(proj(d_model)).astype(BF16)
    gr_ref[...] = jax.nn.sigmoid(proj(d_model)).astype(BF16)


def _in_proj(x2d, mod3, mod_row, norm_g, w_in, q_g, k_g, ret_tabs, att_tabs, *, dims, seq_len,
             emit_cache):
    n_heads, n_kv, hd, r_heads, r_dk, r_dv, d_model = dims
    t_rows = x2d.shape[0]
    tm = ROW_TILE
    rope_qk = att_tabs is not None
    tab_blocks = ret_tabs[0].shape[0] // tm

    def row(i):
        return (i, 0)

    def tab(i):
        return (i % tab_blocks, 0)

    in_specs = [pl.BlockSpec((tm, d_model), row),
                pl.BlockSpec((1, 1, d_model), lambda i: (mod_row(i), 0, 0)),
                pl.BlockSpec((1, 1, d_model), lambda i: (mod_row(i), 0, 1)),
                _resident((1, d_model)),
                _resident(w_in.shape),
                _resident((1, hd)), _resident((1, hd))]
    args = [x2d, mod3, mod3, norm_g.reshape(1, d_model), w_in, q_g.reshape(1, hd), k_g.reshape(1, hd)]
    for tb in ret_tabs:
        in_specs.append(pl.BlockSpec((tm, 128), tab))
        args.append(tb)
    if rope_qk:
        for tb in att_tabs:
            in_specs.append(pl.BlockSpec((tm, hd), tab))
            args.append(tb)
    widths = [n_heads * hd, n_kv * hd, n_kv * hd, r_heads * r_dk, r_heads * r_dk,
              r_heads * r_dv, r_heads * r_dv, d_model, d_model]
    out_specs = [pl.BlockSpec((tm, w), row) for w in widths]
    out_shape = [jax.ShapeDtypeStruct((t_rows, w), BF16) for w in widths]
    if emit_cache:
        nb = t_rows // seq_len
        for _ in range(2):
            out_specs.append(pl.BlockSpec((tm // seq_len, 1, n_kv, seq_len, hd),
                                          lambda i: (i, 0, 0, 0, 0)))
            out_shape.append(jax.ShapeDtypeStruct((nb, 1, n_kv, seq_len, hd), F32))
    return pl.pallas_call(
        functools.partial(_in_proj_kernel, dims=dims, rope_qk=rope_qk, emit_cache=emit_cache,
                          seq_len=seq_len),
        grid=(t_rows // tm,),
        in_specs=in_specs, out_specs=out_specs, out_shape=out_shape,
        compiler_params=_params(1),
        name="in_proj",
    )(*args)


def _attention_kernel(*refs, group, hd, seq_len, has_ctx):
    if has_ctx:
        q_ref, k_ref, v_ref, ck_ref, cv_ref, o_ref, m_scr, l_scr, acc_scr = refs
    else:
        q_ref, k_ref, v_ref, o_ref, m_scr, l_scr, acc_scr = refs
    tq = q_ref.shape[0]
    q = jnp.concatenate([q_ref[:, g * hd:(g + 1) * hd] for g in range(group)], axis=0)
    m_scr[...] = jnp.full(m_scr.shape, -jnp.inf, F32)
    l_scr[...] = jnp.zeros(l_scr.shape, F32)
    acc_scr[...] = jnp.zeros(acc_scr.shape, F32)

    def step(kc, vc):
        s = lax.dot_general(q, kc, (((1,), (1,)), ((), ())), preferred_element_type=F32)
        m_prev = m_scr[...]
        m_new = jnp.maximum(m_prev, jnp.max(s, axis=-1, keepdims=True))
        alpha = jnp.exp(m_prev - m_new)
        p = jnp.exp(s - m_new)
        l_scr[...] = alpha * l_scr[...] + jnp.sum(p, axis=-1, keepdims=True)
        acc_scr[...] = alpha * acc_scr[...] + jnp.dot(p.astype(BF16), vc, preferred_element_type=F32)
        m_scr[...] = m_new

    tk = min(KV_TILE, seq_len)

    def body(c, carry):
        off = pl.multiple_of(c * tk, tk)
        step(k_ref[pl.ds(off, tk), :], v_ref[pl.ds(off, tk), :])
        return carry

    lax.fori_loop(0, seq_len // tk, body, 0)
    if has_ctx:
        step(ck_ref[0, 0, 0], cv_ref[0, 0, 0])
    out = acc_scr[...] / l_scr[...]
    for g in range(group):
        o_ref[:, g * hd:(g + 1) * hd] = out[g * tq:(g + 1) * tq].astype(BF16)


def _attention(q, k, v, ctx, layer, *, batch, seq_len, n_kv, group, hd):
    t_rows = q.shape[0]
    tq = min(Q_TILE, seq_len)
    nq = seq_len // tq
    has_ctx = ctx is not None
    in_specs = [pl.BlockSpec((tq, group * hd), lambda b, h, i: (b * nq + i, h)),
                pl.BlockSpec((seq_len, hd), lambda b, h, i: (b, h)),
                pl.BlockSpec((seq_len, hd), lambda b, h, i: (b, h))]
    args = [q, k, v]
    if has_ctx:
        past = ctx[0].shape[3]
        for a in ctx:
            in_specs.append(pl.BlockSpec((1, 1, 1, past, hd), lambda b, h, i: (b, layer, h, 0, 0)))
            args.append(a)
    return pl.pallas_call(
        functools.partial(_attention_kernel, group=group, hd=hd, seq_len=seq_len, has_ctx=has_ctx),
        grid=(batch, n_kv, nq),
        in_specs=in_specs,
        out_specs=pl.BlockSpec((tq, group * hd), lambda b, h, i: (b * nq + i, h)),
        out_shape=jax.ShapeDtypeStruct((t_rows, n_kv * group * hd), BF16),
        scratch_shapes=[pltpu.VMEM((group * tq, 1), F32), pltpu.VMEM((group * tq, 1), F32),
                        pltpu.VMEM((group * tq, hd), F32)],
        compiler_params=_params(3),
        name="attention",
    )(*args)


def _log_sigmoid(x):
    return jnp.minimum(x, 0.0) - jnp.log1p(jnp.exp(-jnp.abs(x)))


def _retention_kernel(*refs, r_dk, r_dv, seq_len, has_init, emit_state):
    it = iter(refs)
    rq_ref, rk_ref, rv_ref, rg_ref, dk_ref, dv_ref, dc_ref = (next(it) for _ in range(7))
    if has_init:
        s0f_ref, s0b_ref = next(it), next(it)
    o_ref = next(it)
    if emit_state:
        sf_ref, sb_ref = next(it), next(it)
    sprev_scr = next(it)

    c = RET_CHUNK
    n_chunks = seq_len // c
    kw, vw = 2 * r_dk, 2 * r_dv

    lg_k = _log_sigmoid(dk_ref[...])
    lg_v = _log_sigmoid(dv_ref[...])
    lg_c = _log_sigmoid(dc_ref[...])
    idx = lax.broadcasted_iota(jnp.int32, (c, 1), 0).astype(F32)
    q_dec_f = jnp.exp(lg_k[0:1] * (idx + 1.0))
    q_dec_b = jnp.exp(lg_k[1:2] * (c - idx))
    k_dec_f = jnp.exp(lg_k[0:1] * (c - 1.0 - idx))
    k_dec_b = jnp.exp(lg_k[1:2] * idx)
    chunk_f = jnp.exp(lg_v[0:1] * float(c))
    chunk_b = jnp.exp(lg_v[1:2] * float(c))
    diff = (lax.broadcasted_iota(jnp.int32, (c, c), 0)
            - lax.broadcasted_iota(jnp.int32, (c, c), 1)).astype(F32)

    def both_ways(head):
        lf = lg_c[0:1, head * c:(head + 1) * c]
        lb = lg_c[1:2, head * c:(head + 1) * c]
        fwd = jnp.where(diff >= 0, jnp.exp(lf * jnp.maximum(diff, 0.0)), 0.0)
        bwd = jnp.where(diff <= 0, jnp.exp(lb * jnp.maximum(-diff, 0.0)), 0.0)
        return fwd + bwd

    dmat = [both_ways(0), both_ways(1)]
    same_head = ((lax.broadcasted_iota(jnp.int32, (kw, vw), 0) // r_dk)
                 == (lax.broadcasted_iota(jnp.int32, (kw, vw), 1) // r_dv))
    k_lane = lax.broadcasted_iota(jnp.int32, (c, kw), 1)

    def load_state(ref):
        top = jnp.concatenate([ref[0, 0, 0], jnp.zeros((r_dk, r_dv), F32)], axis=1)
        bot = jnp.concatenate([jnp.zeros((r_dk, r_dv), F32), ref[0, 0, 1]], axis=1)
        return jnp.concatenate([top, bot], axis=0)

    def store_state(ref, s):
        ref[0, 0, 0] = s[:r_dk, :r_dv]
        ref[0, 0, 1] = s[r_dk:, r_dv:]

    def outer_kv(kd, v):
        kv = lax.dot_general(kd.astype(BF16), v, (((0,), (0,)), ((), ())), preferred_element_type=F32)
        return jnp.where(same_head, kv, 0.0)

    def rows(n):
        return pl.ds(pl.multiple_of(n * c, c), c)

    def fwd_body(n, s):
        sprev_scr[n] = s
        k = rk_ref[rows(n), :].astype(F32)
        return chunk_f * s + outer_kv(k * k_dec_f, rv_ref[rows(n), :])

    s_f = load_state(s0f_ref) if has_init else jnp.zeros((kw, vw), F32)
    s_f = lax.fori_loop(0, n_chunks, fwd_body, s_f)
    if emit_state:
        store_state(sf_ref, s_f)

    def bwd_body(j, s_b):
        n = n_chunks - 1 - j
        q = rq_ref[rows(n), :]
        k = rk_ref[rows(n), :]
        v = rv_ref[rows(n), :]
        zero = jnp.zeros_like(q)
        q2 = jnp.concatenate([jnp.where(k_lane < r_dk, q, zero), jnp.where(k_lane >= r_dk, q, zero)], axis=0)
        sc = lax.dot_general(q2, k, (((1,), (1,)), ((), ())), preferred_element_type=F32)
        intra = jnp.concatenate(
            [jnp.dot((sc[h * c:(h + 1) * c] * dmat[h]).astype(BF16), v[:, h * r_dv:(h + 1) * r_dv],
                     preferred_element_type=F32) for h in range(2)], axis=1)
        qf = q.astype(F32)
        q_both = jnp.concatenate([(qf * q_dec_f).astype(BF16), (qf * q_dec_b).astype(BF16)], axis=1)
        s_both = jnp.concatenate([sprev_scr[n], s_b], axis=0).astype(BF16)
        o = intra + jnp.dot(q_both, s_both, preferred_element_type=F32)
        o = jnp.concatenate([_rms(o[:, h * r_dv:(h + 1) * r_dv]) for h in range(2)], axis=1)
        o_ref[rows(n), :] = (rg_ref[rows(n), :].astype(F32) * o).astype(BF16)
        return chunk_b * s_b + outer_kv(k.astype(F32) * k_dec_b, v)

    s_b = load_state(s0b_ref) if has_init else jnp.zeros((kw, vw), F32)
    s_b = lax.fori_loop(0, n_chunks, bwd_body, s_b)
    if emit_state:
        store_state(sb_ref, s_b)


def _retention(rq, rk, rv, rg, decays, init, layer, *, batch, seq_len, r_heads, r_dk, r_dv, emit_state):
    t_rows = rq.shape[0]
    pairs = r_heads // 2
    kw, vw = 2 * r_dk, 2 * r_dv
    has_init = init is not None
    in_specs = [pl.BlockSpec((seq_len, kw), lambda b, p: (b, p)),
                pl.BlockSpec((seq_len, kw), lambda b, p: (b, p)),
                pl.BlockSpec((seq_len, vw), lambda b, p: (b, p)),
                pl.BlockSpec((seq_len, vw), lambda b, p: (b, p)),
                pl.BlockSpec((2, kw), lambda b, p: (0, p)),
                pl.BlockSpec((2, vw), lambda b, p: (0, p)),
                pl.BlockSpec((2, 2 * RET_CHUNK), lambda b, p: (0, p))]
    args = [rq, rk, rv, rg, *decays]
    state_spec = pl.BlockSpec((1, 1, 2, r_dk, r_dv), lambda b, p: (b, layer if has_init else 0, p, 0, 0))
    if has_init:
        in_specs += [state_spec, state_spec]
        args += list(init)
    out_specs = [pl.BlockSpec((seq_len, vw), lambda b, p: (b, p))]
    out_shape = [jax.ShapeDtypeStruct((t_rows, r_heads * r_dv), BF16)]
    if emit_state:
        new_spec = pl.BlockSpec((1, 1, 2, r_dk, r_dv), lambda b, p: (b, 0, p, 0, 0))
        out_specs += [new_spec, new_spec]
        out_shape += [jax.ShapeDtypeStruct((batch, 1, r_heads, r_dk, r_dv), F32)] * 2
    return pl.pallas_call(
        functools.partial(_retention_kernel, r_dk=r_dk, r_dv=r_dv, seq_len=seq_len, has_init=has_init,
                          emit_state=emit_state),
        grid=(batch, pairs),
        in_specs=in_specs, out_specs=out_specs, out_shape=out_shape,
        scratch_shapes=[pltpu.VMEM((seq_len // RET_CHUNK, kw, vw), F32)],
        compiler_params=_params(2),
        name="retention",
    )(*args)


def _merge_kernel(ya_ref, yr_ref, ga_ref, gr_ref, x_ref, g1_ref, sh_ref, sc_ref, n2_ref,
                  wa_ref, wr_ref, wo_ref, x1_ref, h2_ref):
    y = (ga_ref[...].astype(F32) * jnp.dot(ya_ref[...], wa_ref[...], preferred_element_type=F32)
         + gr_ref[...].astype(F32) * jnp.dot(yr_ref[...], wr_ref[...], preferred_element_type=F32))
    x1 = x_ref[...] + g1_ref[0] * jnp.dot(y.astype(BF16), wo_ref[...], preferred_element_type=F32)
    x1_ref[...] = x1
    h2 = _rms(x1) * n2_ref[...]
    h2_ref[...] = (h2 * (1.0 + sc_ref[0]) + sh_ref[0]).astype(BF16)


def _merge(ya, yr, ga, gr, x2d, mod3, mod_row, norm2_g, wa, wr, wo):
    t_rows, d = x2d.shape
    tm = ROW_TILE

    def row(i):
        return (i, 0)

    def mod(j):
        return pl.BlockSpec((1, 1, d), lambda i: (mod_row(i), 0, j))

    act = pl.BlockSpec((tm, d), row)
    return pl.pallas_call(
        _merge_kernel,
        grid=(t_rows // tm,),
        in_specs=[act, act, act, act, act, mod(2), mod(3), mod(4), _resident((1, d)),
                  _resident(wa.shape), _resident(wr.shape), _resident(wo.shape)],
        out_specs=[act, act],
        out_shape=[jax.ShapeDtypeStruct((t_rows, d), F32), jax.ShapeDtypeStruct((t_rows, d), BF16)],
        compiler_params=_params(1),
        name="merge",
    )(ya, yr, ga, gr, x2d, mod3, mod3, mod3, norm2_g.reshape(1, d), wa, wr, wo)


def _ffn_kernel(h_ref, hp_ref, hn_ref, x_ref, g2_ref, wu_ref, cw_ref, cb_ref, wd_ref, fg_ref,
                o_ref, u_scr, *, d_ff, seq_len, final_norm):
    tm = h_ref.shape[0]
    h_ext = jnp.concatenate([hp_ref[...], h_ref[...], hn_ref[...]], axis=0)
    pos = (pl.program_id(0) * tm + lax.broadcasted_iota(jnp.int32, (tm, 1), 0)) % seq_len
    has_prev = pos != 0
    has_next = pos != seq_len - 1
    n_split = 2
    ck = d_ff // n_split
    acc = None
    for j in range(n_split):
        halves = []
        for base in (j * ck, d_ff + j * ck):
            u_scr[...] = jnp.dot(h_ext, wu_ref[:, base:base + ck], preferred_element_type=F32)
            prev = jnp.where(has_prev, u_scr[pl.ds(HALO - 1, tm), :], 0.0)
            here = u_scr[pl.ds(HALO, tm), :]
            nxt = jnp.where(has_next, u_scr[pl.ds(HALO + 1, tm), :], 0.0)
            cols = slice(base, base + ck)
            halves.append(cw_ref[0:1, cols] * prev + cw_ref[1:2, cols] * here
                          + cw_ref[2:3, cols] * nxt + cb_ref[:, cols])
        a, g = halves
        act = (g * jax.nn.sigmoid(g) * a).astype(BF16)
        part = jnp.dot(act, wd_ref[j * ck:(j + 1) * ck, :], preferred_element_type=F32)
        acc = part if acc is None else acc + part
    x2 = x_ref[...] + g2_ref[0] * acc
    if final_norm:
        x2 = _rms(x2) * fg_ref[...]
    o_ref[...] = x2


def _conv_ffn(h2, x1, mod3, mod_row, w_up, conv_w, conv_b, w_down, final_g, *, seq_len, final_norm):
    t_rows, d = x1.shape
    d_ff = w_down.shape[0]
    tm = ROW_TILE
    per = tm // HALO
    last = t_rows // HALO - 1

    def row(i):
        return (i, 0)

    return pl.pallas_call(
        functools.partial(_ffn_kernel, d_ff=d_ff, seq_len=seq_len, final_norm=final_norm),
        grid=(t_rows // tm,),
        in_specs=[pl.BlockSpec((tm, d), row),
                  pl.BlockSpec((HALO, d), lambda i: (jnp.maximum(i * per - 1, 0), 0)),
                  pl.BlockSpec((HALO, d), lambda i: (jnp.minimum((i + 1) * per, last), 0)),
                  pl.BlockSpec((tm, d), row),
                  pl.BlockSpec((1, 1, d), lambda i: (mod_row(i), 0, 5)),
                  _resident(w_up.shape), _resident(conv_w.shape), _resident((1, 2 * d_ff)),
                  _resident(w_down.shape), _resident((1, d))],
        out_specs=pl.BlockSpec((tm, d), row),
        out_shape=jax.ShapeDtypeStruct((t_rows, d), F32),
        scratch_shapes=[pltpu.VMEM((tm + 2 * HALO, d_ff // 2), F32)],
        compiler_params=_params(1),
        name="conv_ffn",
    )(h2, h2, h2, x1, mod3, w_up, conv_w, conv_b.reshape(1, 2 * d_ff), w_down, final_g.reshape(1, d))


def kernel(x_prompt, x_sample, c, cache_k, cache_v, state_ret_fwd, state_ret_bwd, c_ctx, norm1_g, norm2_g, w_mod, b_mod, w_in, q_norm_g, k_norm_g, decay_fwd, decay_bwd, w_att_o, w_ret_o, w_out, w_up, conv_w, conv_b, w_down, final_g):
    batch_p, seq_p, d_model = x_prompt.shape
    batch_s, seq_s, _ = x_sample.shape
    depth = w_in.shape[0]
    n_kv, past, hd = cache_k.shape[2], cache_k.shape[3], cache_k.shape[4]
    r_heads, r_dk, r_dv = state_ret_fwd.shape[2], state_ret_fwd.shape[3], state_ret_fwd.shape[4]
    n_heads = w_att_o.shape[1] // hd
    group = n_heads // n_kv
    dims = (n_heads, n_kv, hd, r_heads, r_dk, r_dv, d_model)
    assert seq_s % GRID_W == 0 and seq_s % ROW_TILE == 0 and ROW_TILE % seq_p == 0
    assert hd == 128 and 2 * r_dk == 128 and r_heads % 2 == 0
    assert seq_p % RET_CHUNK == 0 and seq_s % RET_CHUNK == 0

    xp = x_prompt.reshape(batch_p * seq_p, d_model)
    xs = x_sample.reshape(batch_s * seq_s, d_model)
    cond = jnp.concatenate([c_ctx[None], c, jnp.zeros((8 - 1 - batch_s, d_model), F32)], axis=0)

    pos_p = jnp.tile(jnp.arange(seq_p), ROW_TILE // seq_p)
    ret_tabs_p = _rope_tables(pos_p, 128)
    ret_tabs_s = _rope_tables(past + jnp.arange(seq_s), 128)
    t = jnp.arange(seq_s)
    row_t, col_t = _rope_tables(t // GRID_W, 64), _rope_tables(t % GRID_W, 64)
    att_tabs_s = tuple(jnp.concatenate([a, b], axis=1) for a, b in zip(row_t, col_t))

    tiles_per_seq = seq_s // ROW_TILE

    def mod_row_p(i):
        return 0

    def mod_row_s(i):
        return 1 + i // tiles_per_seq

    ck_bf, cv_bf = cache_k.astype(BF16), cache_v.astype(BF16)
    new_k, new_v, new_sf, new_sb = [], [], [], []
    for l in range(depth):
        last = l == depth - 1
        mod3 = _modulation(cond, w_mod[l], b_mod[l]).reshape(8, 1, 6 * d_model)
        w_in_b, wa_b, wr_b, wo_b = (w[l].astype(BF16) for w in (w_in, w_att_o, w_ret_o, w_out))
        wu_b, wd_b = w_up[l].astype(BF16), w_down[l].astype(BF16)
        decays = tuple(jnp.stack([jnp.repeat(decay_fwd[l], r), jnp.repeat(decay_bwd[l], r)])
                       for r in (r_dk, r_dv, RET_CHUNK))

        def layer(x2d, mod_row, ret_tabs, att_tabs, ctx, init, batch, seq_len, is_prompt):
            outs = _in_proj(x2d, mod3, mod_row, norm1_g[l], w_in_b, q_norm_g[l], k_norm_g[l], ret_tabs,
                            att_tabs, dims=dims, seq_len=seq_len, emit_cache=is_prompt)
            q, k, v, rq, rk, rv, rg, ga, gr = outs[:9]
            ya = _attention(q, k, v, ctx, l, batch=batch, seq_len=seq_len, n_kv=n_kv, group=group, hd=hd)
            ret = _retention(rq, rk, rv, rg, decays, init, l, batch=batch, seq_len=seq_len,
                             r_heads=r_heads, r_dk=r_dk, r_dv=r_dv, emit_state=is_prompt)
            x1, h2 = _merge(ya, ret[0], ga, gr, x2d, mod3, mod_row, norm2_g[l], wa_b, wr_b, wo_b)
            x2 = _conv_ffn(h2, x1, mod3, mod_row, wu_b, conv_w[l], conv_b[l], wd_b, final_g,
                           seq_len=seq_len, final_norm=last)
            return x2, outs[9:], ret[1:]

        xp, cache_new, state_new = layer(xp, mod_row_p, ret_tabs_p, None, None, None, batch_p, seq_p, True)
        new_k.append(cache_new[0])
        new_v.append(cache_new[1])
        new_sf.append(state_new[0])
        new_sb.append(state_new[1])
        xs, _, _ = layer(xs, mod_row_s, ret_tabs_s, att_tabs_s, (ck_bf, cv_bf),
                         (state_ret_fwd, state_ret_bwd), batch_s, seq_s, False)

    def stack(parts):
        return parts[0] if depth == 1 else jnp.concatenate(parts, axis=1)

    return (xp.reshape(batch_p, seq_p, d_model), xs.reshape(batch_s, seq_s, d_model),
            stack(new_k), stack(new_v), stack(new_sf), stack(new_sb))
```

```python
import functools

import jax
import jax.numpy as jnp
import numpy as np
from jax import lax
from jax.experimental import pallas as pl
from jax.experimental.pallas import tpu as pltpu

GRID_W = 64
ROPE_THETA = 10000.0
EPS = 1e-6
RET_CHUNK = 256
ROW_TILE = 512
Q_TILE = 256
KV_TILE = 512
HALO = 16
VMEM_LIMIT = 56 * 1024 * 1024

F32 = jnp.float32
BF16 = jnp.bfloat16


def _params(n_axes):
    return pltpu.CompilerParams(dimension_semantics=("arbitrary",) * n_axes,
                                vmem_limit_bytes=VMEM_LIMIT)


def _resident(shape):
    nd = len(shape)
    return pl.BlockSpec(shape, lambda *_: (0,) * nd, pipeline_mode=pl.Buffered(1))


def _rms(x):
    return x * lax.rsqrt(jnp.mean(x * x, axis=-1, keepdims=True) + EPS)


def _rotate(t, cos, sin_lo, sin_hi):
    return t * cos + pltpu.roll(t, 96, 1) * sin_lo + pltpu.roll(t, 32, 1) * sin_hi


def _rope_tables(pos, width):
    half = 32
    inv = ROPE_THETA ** (-jnp.arange(half, dtype=F32) / half)
    ang = pos.astype(F32)[:, None] * inv[None, :]
    cos, sin = jnp.cos(ang), jnp.sin(ang)
    zero = jnp.zeros_like(sin)
    reps = width // 64
    cos_t = jnp.tile(jnp.concatenate([cos, cos], axis=1), (1, reps))
    sin_lo = jnp.tile(jnp.concatenate([-sin, zero], axis=1), (1, reps))
    sin_hi = jnp.tile(jnp.concatenate([zero, sin], axis=1), (1, reps))
    return cos_t, sin_lo, sin_hi


def _mod_kernel(cond_ref, w_ref, b_ref, o_ref):
    cnd = cond_ref[...]
    s = (cnd * jax.nn.sigmoid(cnd)).astype(BF16)
    o_ref[...] = jnp.dot(s, w_ref[...].astype(BF16), preferred_element_type=F32) + b_ref[...]


def _modulation(cond, w_mod, b_mod):
    rows, d = cond.shape
    n = w_mod.shape[1]
    tn = 1024
    return pl.pallas_call(
        _mod_kernel,
        grid=(n // tn,),
        in_specs=[pl.BlockSpec((rows, d), lambda j: (0, 0)),
                  pl.BlockSpec((d, tn), lambda j: (0, j)),
                  pl.BlockSpec((1, tn), lambda j: (0, j))],
        out_specs=pl.BlockSpec((rows, tn), lambda j: (0, j)),
        out_shape=jax.ShapeDtypeStruct((rows, n), F32),
        compiler_params=_params(1),
        name="modulation",
    )(cond, w_mod, b_mod.reshape(1, n))


def _in_proj_kernel(*refs, dims, rope_qk, emit_cache, seq_len):
    n_heads, n_kv, hd, r_heads, r_dk, r_dv, d_model = dims
    it = iter(refs)
    x_ref, sh_ref, sc_ref, g_ref, w_ref, qg_ref, kg_ref = (next(it) for _ in range(7))
    rcos_ref, rlo_ref, rhi_ref = (next(it) for _ in range(3))
    if rope_qk:
        acos_ref, alo_ref, ahi_ref = (next(it) for _ in range(3))
    q_ref, k_ref, v_ref, rq_ref, rk_ref, rv_ref, rg_ref, ga_ref, gr_ref = (next(it) for _ in range(9))
    if emit_cache:
        ck_ref, cv_ref = next(it), next(it)

    x = x_ref[...]
    h = _rms(x) * g_ref[...]
    h = (h * (1.0 + sc_ref[0]) + sh_ref[0]).astype(BF16)

    col = [0]

    def proj(width):
        lo = col[0]
        col[0] = lo + width
        return jnp.dot(h, w_ref[:, lo:lo + width], preferred_element_type=F32)

    tm = x.shape[0]
    zq = proj(n_heads * hd)
    for i in range(n_heads):
        t = _rms(zq[:, i * hd:(i + 1) * hd]) * qg_ref[...]
        if rope_qk:
            t = _rotate(t, acos_ref[...], alo_ref[...], ahi_ref[...])
        q_ref[:, i * hd:(i + 1) * hd] = (t * (hd ** -0.5)).astype(BF16)
    zk = proj(n_kv * hd)
    zv = proj(n_kv * hd)
    v_ref[...] = zv.astype(BF16)
    for i in range(n_kv):
        t = _rms(zk[:, i * hd:(i + 1) * hd]) * kg_ref[...]
        if emit_cache:
            for s in range(tm // seq_len):
                ck_ref[s, 0, i] = t[s * seq_len:(s + 1) * seq_len]
                cv_ref[s, 0, i] = zv[s * seq_len:(s + 1) * seq_len, i * hd:(i + 1) * hd]
        if rope_qk:
            t = _rotate(t, acos_ref[...], alo_ref[...], ahi_ref[...])
        k_ref[:, i * hd:(i + 1) * hd] = t.astype(BF16)
    zrq = proj(r_heads * r_dk)
    zrk = proj(r_heads * r_dk)
    for i in range(r_heads * r_dk // 128):
        sl = slice(i * 128, (i + 1) * 128)
        rq_ref[:, sl] = _rotate(zrq[:, sl], rcos_ref[...], rlo_ref[...], rhi_ref[...]).astype(BF16)
        t = _rotate(zrk[:, sl], rcos_ref[...], rlo_ref[...], rhi_ref[...])
        rk_ref[:, sl] = (t * (r_dk ** -0.5)).astype(BF16)
    rv_ref[...] = proj(r_heads * r_dv).astype(BF16)
    zg = proj(r_heads * r_dv)
    rg_ref[...] = (zg * jax.nn.sigmoid(zg)).astype(BF16)
    ga_ref[...] = jax.nn.sigmoid(proj(d_model)).astype(BF16)
    gr_ref[...] = jax.nn.sigmoid(proj(d_model)).astype(BF16)


def _in_proj(x2d, mod3, mod_row, norm_g, w_in, q_g, k_g, ret_tabs, att_tabs, *, dims, seq_len,
             emit_cache):
    n_heads, n_kv, hd, r_heads, r_dk, r_dv, d_model = dims
    t_rows = x2d.shape[0]
    tm = ROW_TILE
    rope_qk = att_tabs is not None
    tab_blocks = ret_tabs[0].shape[0] // tm

    def row(i):
        return (i, 0)

    def tab(i):
        return (i % tab_blocks, 0)

    in_specs = [pl.BlockSpec((tm, d_model), row),
                pl.BlockSpec((1, 1, d_model), lambda i: (mod_row(i), 0, 0)),
                pl.BlockSpec((1, 1, d_model), lambda i: (mod_row(i), 0, 1)),
                _resident((1, d_model)),
                _resident(w_in.shape),
                _resident((1, hd)), _resident((1, hd))]
    args = [x2d, mod3, mod3, norm_g.reshape(1, d_model), w_in, q_g.reshape(1, hd), k_g.reshape(1, hd)]
    for tb in ret_tabs:
        in_specs.append(pl.BlockSpec((tm, 128), tab))
        args.append(tb)
    if rope_qk:
        for tb in att_tabs:
            in_specs.append(pl.BlockSpec((tm, hd), tab))
            args.append(tb)
    widths = [n_heads * hd, n_kv * hd, n_kv * hd, r_heads * r_dk, r_heads * r_dk,
              r_heads * r_dv, r_heads * r_dv, d_model, d_model]
    out_specs = [pl.BlockSpec((tm, w), row) for w in widths]
    out_shape = [jax.ShapeDtypeStruct((t_rows, w), BF16) for w in widths]
    if emit_cache:
        nb = t_rows // seq_len
        for _ in range(2):
            out_specs.append(pl.BlockSpec((tm // seq_len, 1, n_kv, seq_len, hd),
                                          lambda i: (i, 0, 0, 0, 0)))
            out_shape.append(jax.ShapeDtypeStruct((nb, 1, n_kv, seq_len, hd), F32))
    return pl.pallas_call(
        functools.partial(_in_proj_kernel, dims=dims, rope_qk=rope_qk, emit_cache=emit_cache,
                          seq_len=seq_len),
        grid=(t_rows // tm,),
        in_specs=in_specs, out_specs=out_specs, out_shape=out_shape,
        compiler_params=_params(1),
        name="in_proj",
    )(*args)


def _attention_kernel(*refs, group, hd, seq_len, has_ctx):
    if has_ctx:
        q_ref, k_ref, v_ref, ck_ref, cv_ref, o_ref, vt_scr = refs
    else:
        q_ref, k_ref, v_ref, o_ref, vt_scr = refs
    tq = q_ref.shape[0]
    tk = min(KV_TILE, seq_len)

    @pl.when(pl.program_id(2) == 0)
    def _():
        for c in range(seq_len // tk):
            vt_scr[:, c * tk:(c + 1) * tk] = v_ref[c * tk:(c + 1) * tk, :].astype(F32).T.astype(BF16)
        if has_ctx:
            vt_scr[:, seq_len:] = cv_ref[0, 0, 0].astype(F32).T.astype(BF16)

    qt = jnp.concatenate([q_ref[:, g * hd:(g + 1) * hd].astype(F32).T.astype(BF16) for g in range(group)],
                         axis=1)
    m = l = acc = None
    chunks = [(k_ref[c * tk:(c + 1) * tk, :], (c * tk, tk)) for c in range(seq_len // tk)]
    if has_ctx:
        chunks.append((ck_ref[0, 0, 0], (seq_len, ck_ref.shape[3])))
    for kc, (off, width) in chunks:
        st = jnp.dot(kc, qt, preferred_element_type=F32)
        m_c = jnp.max(st, axis=0, keepdims=True)
        m_new = m_c if m is None else jnp.maximum(m, m_c)
        p = jnp.exp(st - m_new)
        l_c = jnp.sum(p, axis=0, keepdims=True)
        pv = jnp.dot(vt_scr[:, off:off + width], p.astype(BF16), preferred_element_type=F32)
        if m is None:
            l, acc = l_c, pv
        else:
            alpha = jnp.exp(m - m_new)
            l = alpha * l + l_c
            acc = alpha * acc + pv
        m = m_new
    out = acc / l
    for g in range(group):
        o_ref[:, g * hd:(g + 1) * hd] = out[:, g * tq:(g + 1) * tq].T.astype(BF16)


def _attention(q, k, v, ctx, layer, *, batch, seq_len, n_kv, group, hd):
    t_rows = q.shape[0]
    tq = min(Q_TILE, seq_len)
    nq = seq_len // tq
    has_ctx = ctx is not None
    in_specs = [pl.BlockSpec((tq, group * hd), lambda b, h, i: (b * nq + i, h)),
                pl.BlockSpec((seq_len, hd), lambda b, h, i: (b, h)),
                pl.BlockSpec((seq_len, hd), lambda b, h, i: (b, h))]
    args = [q, k, v]
    n_keys = seq_len
    if has_ctx:
        past = ctx[0].shape[3]
        n_keys += past
        for a in ctx:
            in_specs.append(pl.BlockSpec((1, 1, 1, past, hd), lambda b, h, i: (b, layer, h, 0, 0)))
            args.append(a)
    return pl.pallas_call(
        functools.partial(_attention_kernel, group=group, hd=hd, seq_len=seq_len, has_ctx=has_ctx),
        grid=(batch, n_kv, nq),
        in_specs=in_specs,
        out_specs=pl.BlockSpec((tq, group * hd), lambda b, h, i: (b * nq + i, h)),
        out_shape=jax.ShapeDtypeStruct((t_rows, n_kv * group * hd), BF16),
        scratch_shapes=[pltpu.VMEM((hd, n_keys), BF16)],
        compiler_params=_params(3),
        name="attention",
    )(*args)


def _log_sigmoid(x):
    return jnp.minimum(x, 0.0) - jnp.log1p(jnp.exp(-jnp.abs(x)))


def _retention_kernel(*refs, r_dk, r_dv, seq_len, has_init, emit_state):
    it = iter(refs)
    rq_ref, rk_ref, rv_ref, rg_ref, dk_ref, dv_ref, dc_ref = (next(it) for _ in range(7))
    if has_init:
        s0f_ref, s0b_ref = next(it), next(it)
    o_ref = next(it)
    if emit_state:
        sf_ref, sb_ref = next(it), next(it)
    sprev_scr = next(it)

    c = RET_CHUNK
    n_chunks = seq_len // c
    kw, vw = 2 * r_dk, 2 * r_dv

    lg_k = _log_sigmoid(dk_ref[...])
    lg_v = _log_sigmoid(dv_ref[...])
    lg_c = _log_sigmoid(dc_ref[...])
    idx = lax.broadcasted_iota(jnp.int32, (c, 1), 0).astype(F32)
    q_dec_f = jnp.exp(lg_k[0:1] * (idx + 1.0))
    q_dec_b = jnp.exp(lg_k[1:2] * (c - idx))
    k_dec_f = jnp.exp(lg_k[0:1] * (c - 1.0 - idx))
    k_dec_b = jnp.exp(lg_k[1:2] * idx)
    chunk_f = jnp.exp(lg_v[0:1] * float(c))
    chunk_b = jnp.exp(lg_v[1:2] * float(c))
    diff = (lax.broadcasted_iota(jnp.int32, (c, c), 0)
            - lax.broadcasted_iota(jnp.int32, (c, c), 1)).astype(F32)

    def both_ways(head):
        lf = lg_c[0:1, head * c:(head + 1) * c]
        lb = lg_c[1:2, head * c:(head + 1) * c]
        fwd = jnp.where(diff >= 0, jnp.exp(lf * jnp.maximum(diff, 0.0)), 0.0)
        bwd = jnp.where(diff <= 0, jnp.exp(lb * jnp.maximum(-diff, 0.0)), 0.0)
        return fwd + bwd

    dmat = [both_ways(0), both_ways(1)]
    same_head = ((lax.broadcasted_iota(jnp.int32, (kw, vw), 0) // r_dk)
                 == (lax.broadcasted_iota(jnp.int32, (kw, vw), 1) // r_dv))
    k_lane = lax.broadcasted_iota(jnp.int32, (c, kw), 1)

    def load_state(ref):
        top = jnp.concatenate([ref[0, 0, 0], jnp.zeros((r_dk, r_dv), F32)], axis=1)
        bot = jnp.concatenate([jnp.zeros((r_dk, r_dv), F32), ref[0, 0, 1]], axis=1)
        return jnp.concatenate([top, bot], axis=0)

    def store_state(ref, s):
        ref[0, 0, 0] = s[:r_dk, :r_dv]
        ref[0, 0, 1] = s[r_dk:, r_dv:]

    def outer_kv(kd, v):
        kv = lax.dot_general(kd.astype(BF16), v, (((0,), (0,)), ((), ())), preferred_element_type=F32)
        return jnp.where(same_head, kv, 0.0)

    def rows(n):
        return pl.ds(pl.multiple_of(n * c, c), c)

    def fwd_body(n, s):
        sprev_scr[n] = s
        k = rk_ref[rows(n), :].astype(F32)
        return chunk_f * s + outer_kv(k * k_dec_f, rv_ref[rows(n), :])

    s_f = load_state(s0f_ref) if has_init else jnp.zeros((kw, vw), F32)
    s_f = lax.fori_loop(0, n_chunks, fwd_body, s_f)
    if emit_state:
        store_state(sf_ref, s_f)

    def bwd_body(j, s_b):
        n = n_chunks - 1 - j
        q = rq_ref[rows(n), :]
        k = rk_ref[rows(n), :]
        v = rv_ref[rows(n), :]
        zero = jnp.zeros_like(q)
        q2 = jnp.concatenate([jnp.where(k_lane < r_dk, q, zero), jnp.where(k_lane >= r_dk, q, zero)], axis=0)
        sc = lax.dot_general(q2, k, (((1,), (1,)), ((), ())), preferred_element_type=F32)
        intra = jnp.concatenate(
            [jnp.dot((sc[h * c:(h + 1) * c] * dmat[h]).astype(BF16), v[:, h * r_dv:(h + 1) * r_dv],
                     preferred_element_type=F32) for h in range(2)], axis=1)
        qf = q.astype(F32)
        q_both = jnp.concatenate([(qf * q_dec_f).astype(BF16), (qf * q_dec_b).astype(BF16)], axis=1)
        s_both = jnp.concatenate([sprev_scr[n], s_b], axis=0).astype(BF16)
        o = intra + jnp.dot(q_both, s_both, preferred_element_type=F32)
        o = jnp.concatenate([_rms(o[:, h * r_dv:(h + 1) * r_dv]) for h in range(2)], axis=1)
        o_ref[rows(n), :] = (rg_ref[rows(n), :].astype(F32) * o).astype(BF16)
        return chunk_b * s_b + outer_kv(k.astype(F32) * k_dec_b, v)

    s_b = load_state(s0b_ref) if has_init else jnp.zeros((kw, vw), F32)
    s_b = lax.fori_loop(0, n_chunks, bwd_body, s_b)
    if emit_state:
        store_state(sb_ref, s_b)


def _retention(rq, rk, rv, rg, decays, init, layer, *, batch, seq_len, r_heads, r_dk, r_dv, emit_state):
    t_rows = rq.shape[0]
    pairs = r_heads // 2
    kw, vw = 2 * r_dk, 2 * r_dv
    has_init = init is not None
    in_specs = [pl.BlockSpec((seq_len, kw), lambda b, p: (b, p)),
                pl.BlockSpec((seq_len, kw), lambda b, p: (b, p)),
                pl.BlockSpec((seq_len, vw), lambda b, p: (b, p)),
                pl.BlockSpec((seq_len, vw), lambda b, p: (b, p)),
                pl.BlockSpec((2, kw), lambda b, p: (0, p)),
                pl.BlockSpec((2, vw), lambda b, p: (0, p)),
                pl.BlockSpec((2, 2 * RET_CHUNK), lambda b, p: (0, p))]
    args = [rq, rk, rv, rg, *decays]
    state_spec = pl.BlockSpec((1, 1, 2, r_dk, r_dv), lambda b, p: (b, layer if has_init else 0, p, 0, 0))
    if has_init:
        in_specs += [state_spec, state_spec]
        args += list(init)
    out_specs = [pl.BlockSpec((seq_len, vw), lambda b, p: (b, p))]
    out_shape = [jax.ShapeDtypeStruct((t_rows, r_heads * r_dv), BF16)]
    if emit_state:
        new_spec = pl.BlockSpec((1, 1, 2, r_dk, r_dv), lambda b, p: (b, 0, p, 0, 0))
        out_specs += [new_spec, new_spec]
        out_shape += [jax.ShapeDtypeStruct((batch, 1, r_heads, r_dk, r_dv), F32)] * 2
    return pl.pallas_call(
        functools.partial(_retention_kernel, r_dk=r_dk, r_dv=r_dv, seq_len=seq_len, has_init=has_init,
                          emit_state=emit_state),
        grid=(batch, pairs),
        in_specs=in_specs, out_specs=out_specs, out_shape=out_shape,
        scratch_shapes=[pltpu.VMEM((seq_len // RET_CHUNK, kw, vw), F32)],
        compiler_params=_params(2),
        name="retention",
    )(*args)


def _merge_kernel(ya_ref, yr_ref, ga_ref, gr_ref, x_ref, g1_ref, sh_ref, sc_ref, n2_ref,
                  wa_ref, wr_ref, wo_ref, x1_ref, h2_ref):
    y = (ga_ref[...].astype(F32) * jnp.dot(ya_ref[...], wa_ref[...], preferred_element_type=F32)
         + gr_ref[...].astype(F32) * jnp.dot(yr_ref[...], wr_ref[...], preferred_element_type=F32))
    x1 = x_ref[...] + g1_ref[0] * jnp.dot(y.astype(BF16), wo_ref[...], preferred_element_type=F32)
    x1_ref[...] = x1
    h2 = _rms(x1) * n2_ref[...]
    h2_ref[...] = (h2 * (1.0 + sc_ref[0]) + sh_ref[0]).astype(BF16)


def _merge(ya, yr, ga, gr, x2d, mod3, mod_row, norm2_g, wa, wr, wo):
    t_rows, d = x2d.shape
    tm = ROW_TILE

    def row(i):
        return (i, 0)

    def mod(j):
        return pl.BlockSpec((1, 1, d), lambda i: (mod_row(i), 0, j))

    act = pl.BlockSpec((tm, d), row)
    return pl.pallas_call(
        _merge_kernel,
        grid=(t_rows // tm,),
        in_specs=[act, act, act, act, act, mod(2), mod(3), mod(4), _resident((1, d)),
                  _resident(wa.shape), _resident(wr.shape), _resident(wo.shape)],
        out_specs=[act, act],
        out_shape=[jax.ShapeDtypeStruct((t_rows, d), F32), jax.ShapeDtypeStruct((t_rows, d), BF16)],
        compiler_params=_params(1),
        name="merge",
    )(ya, yr, ga, gr, x2d, mod3, mod3, mod3, norm2_g.reshape(1, d), wa, wr, wo)


def _ffn_kernel(h_ref, hp_ref, hn_ref, x_ref, g2_ref, wu_ref, cw_ref, cb_ref, wd_ref, fg_ref,
                o_ref, u_scr, *, d_ff, seq_len, final_norm):
    tm = h_ref.shape[0]
    h_ext = jnp.concatenate([hp_ref[...], h_ref[...], hn_ref[...]], axis=0)
    pos = (pl.program_id(0) * tm + lax.broadcasted_iota(jnp.int32, (tm, 1), 0)) % seq_len
    has_prev = pos != 0
    has_next = pos != seq_len - 1
    n_split = 2
    ck = d_ff // n_split
    acc = None
    for j in range(n_split):
        halves = []
        for base in (j * ck, d_ff + j * ck):
            u_scr[...] = jnp.dot(h_ext, wu_ref[:, base:base + ck], preferred_element_type=F32)
            prev = jnp.where(has_prev, u_scr[pl.ds(HALO - 1, tm), :], 0.0)
            here = u_scr[pl.ds(HALO, tm), :]
            nxt = jnp.where(has_next, u_scr[pl.ds(HALO + 1, tm), :], 0.0)
            cols = slice(base, base + ck)
            halves.append(cw_ref[0:1, cols] * prev + cw_ref[1:2, cols] * here
                          + cw_ref[2:3, cols] * nxt + cb_ref[:, cols])
        a, g = halves
        act = (g * jax.nn.sigmoid(g) * a).astype(BF16)
        part = jnp.dot(act, wd_ref[j * ck:(j + 1) * ck, :], preferred_element_type=F32)
        acc = part if acc is None else acc + part
    x2 = x_ref[...] + g2_ref[0] * acc
    if final_norm:
        x2 = _rms(x2) * fg_ref[...]
    o_ref[...] = x2


def _conv_ffn(h2, x1, mod3, mod_row, w_up, conv_w, conv_b, w_down, final_g, *, seq_len, final_norm):
    t_rows, d = x1.shape
    d_ff = w_down.shape[0]
    tm = ROW_TILE
    per = tm // HALO
    last = t_rows // HALO - 1

    def row(i):
        return (i, 0)

    return pl.pallas_call(
        functools.partial(_ffn_kernel, d_ff=d_ff, seq_len=seq_len, final_norm=final_norm),
        grid=(t_rows // tm,),
        in_specs=[pl.BlockSpec((tm, d), row),
                  pl.BlockSpec((HALO, d), lambda i: (jnp.maximum(i * per - 1, 0), 0)),
                  pl.BlockSpec((HALO, d), lambda i: (jnp.minimum((i + 1) * per, last), 0)),
                  pl.BlockSpec((tm, d), row),
                  pl.BlockSpec((1, 1, d), lambda i: (mod_row(i), 0, 5)),
                  _resident(w_up.shape), _resident(conv_w.shape), _resident((1, 2 * d_ff)),
                  _resident(w_down.shape), _resident((1, d))],
        out_specs=pl.BlockSpec((tm, d), row),
        out_shape=jax.ShapeDtypeStruct((t_rows, d), F32),
        scratch_shapes=[pltpu.VMEM((tm + 2 * HALO, d_ff // 2), F32)],
        compiler_params=_params(1),
        name="conv_ffn",
    )(h2, h2, h2, x1, mod3, w_up, conv_w, conv_b.reshape(1, 2 * d_ff), w_down, final_g.reshape(1, d))


def kernel(x_prompt, x_sample, c, cache_k, cache_v, state_ret_fwd, state_ret_bwd, c_ctx, norm1_g, norm2_g, w_mod, b_mod, w_in, q_norm_g, k_norm_g, decay_fwd, decay_bwd, w_att_o, w_ret_o, w_out, w_up, conv_w, conv_b, w_down, final_g):
    batch_p, seq_p, d_model = x_prompt.shape
    batch_s, seq_s, _ = x_sample.shape
    depth = w_in.shape[0]
    n_kv, past, hd = cache_k.shape[2], cache_k.shape[3], cache_k.shape[4]
    r_heads, r_dk, r_dv = state_ret_fwd.shape[2], state_ret_fwd.shape[3], state_ret_fwd.shape[4]
    n_heads = w_att_o.shape[1] // hd
    group = n_heads // n_kv
    dims = (n_heads, n_kv, hd, r_heads, r_dk, r_dv, d_model)
    assert seq_s % GRID_W == 0 and seq_s % ROW_TILE == 0 and ROW_TILE % seq_p == 0
    assert hd == 128 and 2 * r_dk == 128 and r_heads % 2 == 0
    assert seq_p % RET_CHUNK == 0 and seq_s % RET_CHUNK == 0

    xp = x_prompt.reshape(batch_p * seq_p, d_model)
    xs = x_sample.reshape(batch_s * seq_s, d_model)
    cond = jnp.concatenate([c_ctx[None], c, jnp.zeros((8 - 1 - batch_s, d_model), F32)], axis=0)

    pos_p = jnp.tile(jnp.arange(seq_p), ROW_TILE // seq_p)
    ret_tabs_p = _rope_tables(pos_p, 128)
    ret_tabs_s = _rope_tables(past + jnp.arange(seq_s), 128)
    t = jnp.arange(seq_s)
    row_t, col_t = _rope_tables(t // GRID_W, 64), _rope_tables(t % GRID_W, 64)
    att_tabs_s = tuple(jnp.concatenate([a, b], axis=1) for a, b in zip(row_t, col_t))

    tiles_per_seq = seq_s // ROW_TILE

    def mod_row_p(i):
        return 0

    def mod_row_s(i):
        return 1 + i // tiles_per_seq

    ck_bf, cv_bf = cache_k.astype(BF16), cache_v.astype(BF16)
    new_k, new_v, new_sf, new_sb = [], [], [], []
    for l in range(depth):
        last = l == depth - 1
        mod3 = _modulation(cond, w_mod[l], b_mod[l]).reshape(8, 1, 6 * d_model)
        w_in_b, wa_b, wr_b, wo_b = (w[l].astype(BF16) for w in (w_in, w_att_o, w_ret_o, w_out))
        wu_b, wd_b = w_up[l].astype(BF16), w_down[l].astype(BF16)
        decays = tuple(jnp.stack([jnp.repeat(decay_fwd[l], r), jnp.repeat(decay_bwd[l], r)])
                       for r in (r_dk, r_dv, RET_CHUNK))

        def layer(x2d, mod_row, ret_tabs, att_tabs, ctx, init, batch, seq_len, is_prompt):
            outs = _in_proj(x2d, mod3, mod_row, norm1_g[l], w_in_b, q_norm_g[l], k_norm_g[l], ret_tabs,
                            att_tabs, dims=dims, seq_len=seq_len, emit_cache=is_prompt)
            q, k, v, rq, rk, rv, rg, ga, gr = outs[:9]
            ya = _attention(q, k, v, ctx, l, batch=batch, seq_len=seq_len, n_kv=n_kv, group=group, hd=hd)
            ret = _retention(rq, rk, rv, rg, decays, init, l, batch=batch, seq_len=seq_len,
                             r_heads=r_heads, r_dk=r_dk, r_dv=r_dv, emit_state=is_prompt)
            x1, h2 = _merge(ya, ret[0], ga, gr, x2d, mod3, mod_row, norm2_g[l], wa_b, wr_b, wo_b)
            x2 = _conv_ffn(h2, x1, mod3, mod_row, wu_b, conv_w[l], conv_b[l], wd_b, final_g,
                           seq_len=seq_len, final_norm=last)
            return x2, outs[9:], ret[1:]

        xp, cache_new, state_new = layer(xp, mod_row_p, ret_tabs_p, None, None, None, batch_p, seq_p, True)
        new_k.append(cache_new[0])
        new_v.append(cache_new[1])
        new_sf.append(state_new[0])
        new_sb.append(state_new[1])
        xs, _, _ = layer(xs, mod_row_s, ret_tabs_s, att_tabs_s, (ck_bf, cv_bf),
                         (state_ret_fwd, state_ret_bwd), batch_s, seq_s, False)

    def stack(parts):
        return parts[0] if depth == 1 else jnp.concatenate(parts, axis=1)

    return (xp.reshape(batch_p, seq_p, d_model), xs.reshape(batch_s, seq_s, d_model),
            stack(new_k), stack(new_v), stack(new_sf), stack(new_sb))
```

```python
import functools

import jax
import jax.numpy as jnp
import numpy as np
from jax import lax
from jax.experimental import pallas as pl
from jax.experimental.pallas import tpu as pltpu

GRID_W = 64
ROPE_THETA = 10000.0
EPS = 1e-6
RET_CHUNK = 256
ROW_TILE = 512
Q_TILE = 256
KV_TILE = 2048
LOG2_E = float(np.log2(np.e))
HALO = 16
VMEM_LIMIT = 56 * 1024 * 1024

F32 = jnp.float32
BF16 = jnp.bfloat16


def _params(n_axes):
    return pltpu.CompilerParams(dimension_semantics=("arbitrary",) * n_axes,
                                vmem_limit_bytes=VMEM_LIMIT)


def _resident(shape):
    nd = len(shape)
    return pl.BlockSpec(shape, lambda *_: (0,) * nd, pipeline_mode=pl.Buffered(1))


def _rms(x):
    return x * lax.rsqrt(jnp.mean(x * x, axis=-1, keepdims=True) + EPS)


def _rotate(t, cos, sin_lo, sin_hi):
    return t * cos + pltpu.roll(t, 96, 1) * sin_lo + pltpu.roll(t, 32, 1) * sin_hi


def _rope_tables(pos, width):
    half = 32
    inv = ROPE_THETA ** (-jnp.arange(half, dtype=F32) / half)
    ang = pos.astype(F32)[:, None] * inv[None, :]
    cos, sin = jnp.cos(ang), jnp.sin(ang)
    zero = jnp.zeros_like(sin)
    reps = width // 64
    cos_t = jnp.tile(jnp.concatenate([cos, cos], axis=1), (1, reps))
    sin_lo = jnp.tile(jnp.concatenate([-sin, zero], axis=1), (1, reps))
    sin_hi = jnp.tile(jnp.concatenate([zero, sin], axis=1), (1, reps))
    return cos_t, sin_lo, sin_hi


def _mod_kernel(cond_ref, w_ref, b_ref, o_ref):
    cnd = cond_ref[...]
    s = (cnd * jax.nn.sigmoid(cnd)).astype(BF16)
    o_ref[...] = jnp.dot(s, w_ref[...].astype(BF16), preferred_element_type=F32) + b_ref[...]


def _modulation(cond, w_mod, b_mod):
    rows, d = cond.shape
    n = w_mod.shape[1]
    tn = 1024
    return pl.pallas_call(
        _mod_kernel,
        grid=(n // tn,),
        in_specs=[pl.BlockSpec((rows, d), lambda j: (0, 0)),
                  pl.BlockSpec((d, tn), lambda j: (0, j)),
                  pl.BlockSpec((1, tn), lambda j: (0, j))],
        out_specs=pl.BlockSpec((rows, tn), lambda j: (0, j)),
        out_shape=jax.ShapeDtypeStruct((rows, n), F32),
        compiler_params=_params(1),
        name="modulation",
    )(cond, w_mod, b_mod.reshape(1, n))


def _in_proj_kernel(*refs, dims, rope_qk, emit_cache, seq_len):
    n_heads, n_kv, hd, r_heads, r_dk, r_dv, d_model = dims
    it = iter(refs)
    x_ref, sh_ref, sc_ref, g_ref, w_ref, qg_ref, kg_ref = (next(it) for _ in range(7))
    rcos_ref, rlo_ref, rhi_ref = (next(it) for _ in range(3))
    if rope_qk:
        acos_ref, alo_ref, ahi_ref = (next(it) for _ in range(3))
    q_ref, k_ref, v_ref, rq_ref, rk_ref, rv_ref, rg_ref, ga_ref, gr_ref = (next(it) for _ in range(9))
    if emit_cache:
        ck_ref, cv_ref = next(it), next(it)

    x = x_ref[...]
    h = _rms(x) * g_ref[...]
    h = (h * (1.0 + sc_ref[0]) + sh_ref[0]).astype(BF16)

    col = [0]

    def proj(width):
        lo = col[0]
        col[0] = lo + width
        return jnp.dot(h, w_ref[:, lo:lo + width], preferred_element_type=F32)

    tm = x.shape[0]
    zq = proj(n_heads * hd)
    for i in range(n_heads):
        t = _rms(zq[:, i * hd:(i + 1) * hd]) * qg_ref[...]
        if rope_qk:
            t = _rotate(t, acos_ref[...], alo_ref[...], ahi_ref[...])
        q_ref[:, i * hd:(i + 1) * hd] = (t * (hd ** -0.5 * LOG2_E)).astype(BF16)
    zk = proj(n_kv * hd)
    zv = proj(n_kv * hd)
    v_ref[...] = zv.astype(BF16)
    for i in range(n_kv):
        t = _rms(zk[:, i * hd:(i + 1) * hd]) * kg_ref[...]
        if emit_cache:
            for s in range(tm // seq_len):
                ck_ref[s, 0, i] = t[s * seq_len:(s + 1) * seq_len]
                cv_ref[s, 0, i] = zv[s * seq_len:(s + 1) * seq_len, i * hd:(i + 1) * hd]
        if rope_qk:
            t = _rotate(t, acos_ref[...], alo_ref[...], ahi_ref[...])
        k_ref[:, i * hd:(i + 1) * hd] = t.astype(BF16)
    zrq = proj(r_heads * r_dk)
    zrk = proj(r_heads * r_dk)
    for i in range(r_heads * r_dk // 128):
        sl = slice(i * 128, (i + 1) * 128)
        rq_ref[:, sl] = _rotate(zrq[:, sl], rcos_ref[...], rlo_ref[...], rhi_ref[...]).astype(BF16)
        t = _rotate(zrk[:, sl], rcos_ref[...], rlo_ref[...], rhi_ref[...])
        rk_ref[:, sl] = (t * (r_dk ** -0.5)).astype(BF16)
    rv_ref[...] = proj(r_heads * r_dv).astype(BF16)
    zg = proj(r_heads * r_dv)
    rg_ref[...] = (zg * jax.nn.sigmoid(zg)).astype(BF16)
    ga_ref[...] = jax.nn.sigmoid(proj(d_model)).astype(BF16)
    gr_ref[...] = jax.nn.sigmoid(proj(d_model)).astype(BF16)


def _in_proj(x2d, mod3, mod_row, norm_g, w_in, q_g, k_g, ret_tabs, att_tabs, *, dims, seq_len,
             emit_cache):
    n_heads, n_kv, hd, r_heads, r_dk, r_dv, d_model = dims
    t_rows = x2d.shape[0]
    tm = ROW_TILE
    rope_qk = att_tabs is not None
    tab_blocks = ret_tabs[0].shape[0] // tm

    def row(i):
        return (i, 0)

    def tab(i):
        return (i % tab_blocks, 0)

    in_specs = [pl.BlockSpec((tm, d_model), row),
                pl.BlockSpec((1, 1, d_model), lambda i: (mod_row(i), 0, 0)),
                pl.BlockSpec((1, 1, d_model), lambda i: (mod_row(i), 0, 1)),
                _resident((1, d_model)),
                _resident(w_in.shape),
                _resident((1, hd)), _resident((1, hd))]
    args = [x2d, mod3, mod3, norm_g.reshape(1, d_model), w_in, q_g.reshape(1, hd), k_g.reshape(1, hd)]
    for tb in ret_tabs:
        in_specs.append(pl.BlockSpec((tm, 128), tab))
        args.append(tb)
    if rope_qk:
        for tb in att_tabs:
            in_specs.append(pl.BlockSpec((tm, hd), tab))
            args.append(tb)
    widths = [n_heads * hd, n_kv * hd, n_kv * hd, r_heads * r_dk, r_heads * r_dk,
              r_heads * r_dv, r_heads * r_dv, d_model, d_model]
    out_specs = [pl.BlockSpec((tm, w), row) for w in widths]
    out_shape = [jax.ShapeDtypeStruct((t_rows, w), BF16) for w in widths]
    if emit_cache:
        nb = t_rows // seq_len
        for _ in range(2):
            out_specs.append(pl.BlockSpec((tm // seq_len, 1, n_kv, seq_len, hd),
                                          lambda i: (i, 0, 0, 0, 0)))
            out_shape.append(jax.ShapeDtypeStruct((nb, 1, n_kv, seq_len, hd), F32))
    return pl.pallas_call(
        functools.partial(_in_proj_kernel, dims=dims, rope_qk=rope_qk, emit_cache=emit_cache,
                          seq_len=seq_len),
        grid=(t_rows // tm,),
        in_specs=in_specs, out_specs=out_specs, out_shape=out_shape,
        compiler_params=_params(1),
        name="in_proj",
    )(*args)


def _attention_kernel(*refs, group, hd, seq_len, has_ctx):
    if has_ctx:
        q_ref, k_ref, v_ref, ck_ref, cv_ref, o_ref, vt_scr = refs
    else:
        q_ref, k_ref, v_ref, o_ref, vt_scr = refs
    tq = q_ref.shape[0]
    tk = min(KV_TILE, seq_len)

    @pl.when(pl.program_id(2) == 0)
    def _():
        for c in range(seq_len // tk):
            vt_scr[:hd, c * tk:(c + 1) * tk] = v_ref[c * tk:(c + 1) * tk, :].astype(F32).T.astype(BF16)
        if has_ctx:
            vt_scr[:hd, seq_len:] = cv_ref[0, 0, 0].astype(F32).T.astype(BF16)
        vt_scr[hd:, :] = jnp.ones((vt_scr.shape[0] - hd, vt_scr.shape[1]), BF16)

    qt = jnp.concatenate([q_ref[:, g * hd:(g + 1) * hd].astype(F32).T.astype(BF16) for g in range(group)],
                         axis=1)
    m = acc = None
    chunks = [(k_ref[c * tk:(c + 1) * tk, :], (c * tk, tk)) for c in range(seq_len // tk)]
    if has_ctx:
        chunks.append((ck_ref[0, 0, 0], (seq_len, ck_ref.shape[3])))
    for kc, (off, width) in chunks:
        st = jnp.dot(kc, qt, preferred_element_type=F32)
        m_c = jnp.max(st, axis=0, keepdims=True)
        m_new = m_c if m is None else jnp.maximum(m, m_c)
        p = jnp.exp2(st - m_new).astype(BF16)
        pv = jnp.dot(vt_scr[:, off:off + width], p, preferred_element_type=F32)
        acc = pv if m is None else jnp.exp2(m - m_new) * acc + pv
        m = m_new
    out = acc[:hd] / acc[hd:hd + 1]
    for g in range(group):
        o_ref[:, g * hd:(g + 1) * hd] = out[:, g * tq:(g + 1) * tq].T.astype(BF16)


def _attention(q, k, v, ctx, layer, *, batch, seq_len, n_kv, group, hd):
    t_rows = q.shape[0]
    tq = min(Q_TILE, seq_len)
    nq = seq_len // tq
    has_ctx = ctx is not None
    in_specs = [pl.BlockSpec((tq, group * hd), lambda b, h, i: (b * nq + i, h)),
                pl.BlockSpec((seq_len, hd), lambda b, h, i: (b, h)),
                pl.BlockSpec((seq_len, hd), lambda b, h, i: (b, h))]
    args = [q, k, v]
    n_keys = seq_len
    if has_ctx:
        past = ctx[0].shape[3]
        n_keys += past
        for a in ctx:
            in_specs.append(pl.BlockSpec((1, 1, 1, past, hd), lambda b, h, i: (b, layer, h, 0, 0)))
            args.append(a)
    return pl.pallas_call(
        functools.partial(_attention_kernel, group=group, hd=hd, seq_len=seq_len, has_ctx=has_ctx),
        grid=(batch, n_kv, nq),
        in_specs=in_specs,
        out_specs=pl.BlockSpec((tq, group * hd), lambda b, h, i: (b * nq + i, h)),
        out_shape=jax.ShapeDtypeStruct((t_rows, n_kv * group * hd), BF16),
        scratch_shapes=[pltpu.VMEM((hd + HALO, n_keys), BF16)],
        compiler_params=_params(3),
        name="attention",
    )(*args)


def _log_sigmoid(x):
    return jnp.minimum(x, 0.0) - jnp.log1p(jnp.exp(-jnp.abs(x)))


def _retention_kernel(*refs, r_dk, r_dv, seq_len, has_init, emit_state):
    it = iter(refs)
    rq_ref, rk_ref, rv_ref, rg_ref, dk_ref, dv_ref, dc_ref = (next(it) for _ in range(7))
    if has_init:
        s0f_ref, s0b_ref = next(it), next(it)
    o_ref = next(it)
    if emit_state:
        sf_ref, sb_ref = next(it), next(it)
    sprev_scr = next(it)

    c = RET_CHUNK
    n_chunks = seq_len // c
    kw, vw = 2 * r_dk, 2 * r_dv

    lg_k = _log_sigmoid(dk_ref[...])
    lg_v = _log_sigmoid(dv_ref[...])
    lg_c = _log_sigmoid(dc_ref[...])
    idx = lax.broadcasted_iota(jnp.int32, (c, 1), 0).astype(F32)
    q_dec_f = jnp.exp(lg_k[0:1] * (idx + 1.0))
    q_dec_b = jnp.exp(lg_k[1:2] * (c - idx))
    k_dec_f = jnp.exp(lg_k[0:1] * (c - 1.0 - idx))
    k_dec_b = jnp.exp(lg_k[1:2] * idx)
    chunk_f = jnp.exp(lg_v[0:1] * float(c))
    chunk_b = jnp.exp(lg_v[1:2] * float(c))
    diff = (lax.broadcasted_iota(jnp.int32, (c, c), 0)
            - lax.broadcasted_iota(jnp.int32, (c, c), 1)).astype(F32)

    def both_ways(head):
        lf = lg_c[0:1, head * c:(head + 1) * c]
        lb = lg_c[1:2, head * c:(head + 1) * c]
        fwd = jnp.where(diff >= 0, jnp.exp(lf * jnp.maximum(diff, 0.0)), 0.0)
        bwd = jnp.where(diff <= 0, jnp.exp(lb * jnp.maximum(-diff, 0.0)), 0.0)
        return fwd + bwd

    dmat = [both_ways(0), both_ways(1)]
    same_head = ((lax.broadcasted_iota(jnp.int32, (kw, vw), 0) // r_dk)
                 == (lax.broadcasted_iota(jnp.int32, (kw, vw), 1) // r_dv))
    k_lane = lax.broadcasted_iota(jnp.int32, (c, kw), 1)

    def load_state(ref):
        top = jnp.concatenate([ref[0, 0, 0], jnp.zeros((r_dk, r_dv), F32)], axis=1)
        bot = jnp.concatenate([jnp.zeros((r_dk, r_dv), F32), ref[0, 0, 1]], axis=1)
        return jnp.concatenate([top, bot], axis=0)

    def store_state(ref, s):
        ref[0, 0, 0] = s[:r_dk, :r_dv]
        ref[0, 0, 1] = s[r_dk:, r_dv:]

    def outer_kv(kd, v):
        kv = lax.dot_general(kd.astype(BF16), v, (((0,), (0,)), ((), ())), preferred_element_type=F32)
        return jnp.where(same_head, kv, 0.0)

    def rows(n):
        return pl.ds(pl.multiple_of(n * c, c), c)

    def fwd_body(n, s):
        sprev_scr[n] = s
        k = rk_ref[rows(n), :].astype(F32)
        return chunk_f * s + outer_kv(k * k_dec_f, rv_ref[rows(n), :])

    s_f = load_state(s0f_ref) if has_init else jnp.zeros((kw, vw), F32)
    s_f = lax.fori_loop(0, n_chunks, fwd_body, s_f, unroll=min(8, n_chunks))
    if emit_state:
        store_state(sf_ref, s_f)

    def bwd_body(j, s_b):
        n = n_chunks - 1 - j
        q = rq_ref[rows(n), :]
        k = rk_ref[rows(n), :]
        v = rv_ref[rows(n), :]
        zero = jnp.zeros_like(q)
        q2 = jnp.concatenate([jnp.where(k_lane < r_dk, q, zero), jnp.where(k_lane >= r_dk, q, zero)], axis=0)
        sc = lax.dot_general(q2, k, (((1,), (1,)), ((), ())), preferred_element_type=F32)
        intra = jnp.concatenate(
            [jnp.dot((sc[h * c:(h + 1) * c] * dmat[h]).astype(BF16), v[:, h * r_dv:(h + 1) * r_dv],
                     preferred_element_type=F32) for h in range(2)], axis=1)
        qf = q.astype(F32)
        q_both = jnp.concatenate([(qf * q_dec_f).astype(BF16), (qf * q_dec_b).astype(BF16)], axis=1)
        s_both = jnp.concatenate([sprev_scr[n], s_b], axis=0).astype(BF16)
        o = intra + jnp.dot(q_both, s_both, preferred_element_type=F32)
        o = jnp.concatenate([_rms(o[:, h * r_dv:(h + 1) * r_dv]) for h in range(2)], axis=1)
        o_ref[rows(n), :] = (rg_ref[rows(n), :].astype(F32) * o).astype(BF16)
        return chunk_b * s_b + outer_kv(k.astype(F32) * k_dec_b, v)

    s_b = load_state(s0b_ref) if has_init else jnp.zeros((kw, vw), F32)
    s_b = lax.fori_loop(0, n_chunks, bwd_body, s_b, unroll=min(4, n_chunks))
    if emit_state:
        store_state(sb_ref, s_b)


def _retention(rq, rk, rv, rg, decays, init, layer, *, batch, seq_len, r_heads, r_dk, r_dv, emit_state):
    t_rows = rq.shape[0]
    pairs = r_heads // 2
    kw, vw = 2 * r_dk, 2 * r_dv
    has_init = init is not None
    in_specs = [pl.BlockSpec((seq_len, kw), lambda b, p: (b, p)),
                pl.BlockSpec((seq_len, kw), lambda b, p: (b, p)),
                pl.BlockSpec((seq_len, vw), lambda b, p: (b, p)),
                pl.BlockSpec((seq_len, vw), lambda b, p: (b, p)),
                pl.BlockSpec((2, kw), lambda b, p: (0, p)),
                pl.BlockSpec((2, vw), lambda b, p: (0, p)),
                pl.BlockSpec((2, 2 * RET_CHUNK), lambda b, p: (0, p))]
    args = [rq, rk, rv, rg, *decays]
    state_spec = pl.BlockSpec((1, 1, 2, r_dk, r_dv), lambda b, p: (b, layer if has_init else 0, p, 0, 0))
    if has_init:
        in_specs += [state_spec, state_spec]
        args += list(init)
    out_specs = [pl.BlockSpec((seq_len, vw), lambda b, p: (b, p))]
    out_shape = [jax.ShapeDtypeStruct((t_rows, r_heads * r_dv), BF16)]
    if emit_state:
        new_spec = pl.BlockSpec((1, 1, 2, r_dk, r_dv), lambda b, p: (b, 0, p, 0, 0))
        out_specs += [new_spec, new_spec]
        out_shape += [jax.ShapeDtypeStruct((batch, 1, r_heads, r_dk, r_dv), F32)] * 2
    return pl.pallas_call(
        functools.partial(_retention_kernel, r_dk=r_dk, r_dv=r_dv, seq_len=seq_len, has_init=has_init,
                          emit_state=emit_state),
        grid=(batch, pairs),
        in_specs=in_specs, out_specs=out_specs, out_shape=out_shape,
        scratch_shapes=[pltpu.VMEM((seq_len // RET_CHUNK, kw, vw), F32)],
        compiler_params=_params(2),
        name="retention",
    )(*args)


def _merge_kernel(ya_ref, yr_ref, ga_ref, gr_ref, x_ref, g1_ref, sh_ref, sc_ref, n2_ref,
                  wa_ref, wr_ref, wo_ref, x1_ref, h2_ref):
    y = (ga_ref[...].astype(F32) * jnp.dot(ya_ref[...], wa_ref[...], preferred_element_type=F32)
         + gr_ref[...].astype(F32) * jnp.dot(yr_ref[...], wr_ref[...], preferred_element_type=F32))
    x1 = x_ref[...] + g1_ref[0] * jnp.dot(y.astype(BF16), wo_ref[...], preferred_element_type=F32)
    x1_ref[...] = x1
    h2 = _rms(x1) * n2_ref[...]
    h2_ref[...] = (h2 * (1.0 + sc_ref[0]) + sh_ref[0]).astype(BF16)


def _merge(ya, yr, ga, gr, x2d, mod3, mod_row, norm2_g, wa, wr, wo):
    t_rows, d = x2d.shape
    tm = ROW_TILE

    def row(i):
        return (i, 0)

    def mod(j):
        return pl.BlockSpec((1, 1, d), lambda i: (mod_row(i), 0, j))

    act = pl.BlockSpec((tm, d), row)
    return pl.pallas_call(
        _merge_kernel,
        grid=(t_rows // tm,),
        in_specs=[act, act, act, act, act, mod(2), mod(3), mod(4), _resident((1, d)),
                  _resident(wa.shape), _resident(wr.shape), _resident(wo.shape)],
        out_specs=[act, act],
        out_shape=[jax.ShapeDtypeStruct((t_rows, d), F32), jax.ShapeDtypeStruct((t_rows, d), BF16)],
        compiler_params=_params(1),
        name="merge",
    )(ya, yr, ga, gr, x2d, mod3, mod3, mod3, norm2_g.reshape(1, d), wa, wr, wo)


def _ffn_kernel(h_ref, hp_ref, hn_ref, x_ref, g2_ref, wu_ref, cw_ref, cb_ref, wd_ref, fg_ref,
                o_ref, u_scr, *, d_ff, seq_len, final_norm):
    tm = h_ref.shape[0]
    h_ext = jnp.concatenate([hp_ref[...], h_ref[...], hn_ref[...]], axis=0)
    pos = (pl.program_id(0) * tm + lax.broadcasted_iota(jnp.int32, (tm, 1), 0)) % seq_len
    has_prev = pos != 0
    has_next = pos != seq_len - 1
    n_split = 2
    ck = d_ff // n_split
    acc = None
    for j in range(n_split):
        halves = []
        for base in (j * ck, d_ff + j * ck):
            u_scr[...] = jnp.dot(h_ext, wu_ref[:, base:base + ck], preferred_element_type=F32)
            prev = jnp.where(has_prev, u_scr[pl.ds(HALO - 1, tm), :], 0.0)
            here = u_scr[pl.ds(HALO, tm), :]
            nxt = jnp.where(has_next, u_scr[pl.ds(HALO + 1, tm), :], 0.0)
            cols = slice(base, base + ck)
            halves.append(cw_ref[0:1, cols] * prev + cw_ref[1:2, cols] * here
                          + cw_ref[2:3, cols] * nxt + cb_ref[:, cols])
        a, g = halves
        act = (g * jax.nn.sigmoid(g) * a).astype(BF16)
        part = jnp.dot(act, wd_ref[j * ck:(j + 1) * ck, :], preferred_element_type=F32)
        acc = part if acc is None else acc + part
    x2 = x_ref[...] + g2_ref[0] * acc
    if final_norm:
        x2 = _rms(x2) * fg_ref[...]
    o_ref[...] = x2


def _conv_ffn(h2, x1, mod3, mod_row, w_up, conv_w, conv_b, w_down, final_g, *, seq_len, final_norm):
    t_rows, d = x1.shape
    d_ff = w_down.shape[0]
    tm = ROW_TILE
    per = tm // HALO
    last = t_rows // HALO - 1

    def row(i):
        return (i, 0)

    return pl.pallas_call(
        functools.partial(_ffn_kernel, d_ff=d_ff, seq_len=seq_len, final_norm=final_norm),
        grid=(t_rows // tm,),
        in_specs=[pl.BlockSpec((tm, d), row),
                  pl.BlockSpec((HALO, d), lambda i: (jnp.maximum(i * per - 1, 0), 0)),
                  pl.BlockSpec((HALO, d), lambda i: (jnp.minimum((i + 1) * per, last), 0)),
                  pl.BlockSpec((tm, d), row),
                  pl.BlockSpec((1, 1, d), lambda i: (mod_row(i), 0, 5)),
                  _resident(w_up.shape), _resident(conv_w.shape), _resident((1, 2 * d_ff)),
                  _resident(w_down.shape), _resident((1, d))],
        out_specs=pl.BlockSpec((tm, d), row),
        out_shape=jax.ShapeDtypeStruct((t_rows, d), F32),
        scratch_shapes=[pltpu.VMEM((tm + 2 * HALO, d_ff // 2), F32)],
        compiler_params=_params(1),
        name="conv_ffn",
    )(h2, h2, h2, x1, mod3, w_up, conv_w, conv_b.reshape(1, 2 * d_ff), w_down, final_g.reshape(1, d))


def kernel(x_prompt, x_sample, c, cache_k, cache_v, state_ret_fwd, state_ret_bwd, c_ctx, norm1_g, norm2_g, w_mod, b_mod, w_in, q_norm_g, k_norm_g, decay_fwd, decay_bwd, w_att_o, w_ret_o, w_out, w_up, conv_w, conv_b, w_down, final_g):
    batch_p, seq_p, d_model = x_prompt.shape
    batch_s, seq_s, _ = x_sample.shape
    depth = w_in.shape[0]
    n_kv, past, hd = cache_k.shape[2], cache_k.shape[3], cache_k.shape[4]
    r_heads, r_dk, r_dv = state_ret_fwd.shape[2], state_ret_fwd.shape[3], state_ret_fwd.shape[4]
    n_heads = w_att_o.shape[1] // hd
    group = n_heads // n_kv
    dims = (n_heads, n_kv, hd, r_heads, r_dk, r_dv, d_model)
    assert seq_s % GRID_W == 0 and seq_s % ROW_TILE == 0 and ROW_TILE % seq_p == 0
    assert hd == 128 and 2 * r_dk == 128 and r_heads % 2 == 0
    assert seq_p % RET_CHUNK == 0 and seq_s % RET_CHUNK == 0

    xp = x_prompt.reshape(batch_p * seq_p, d_model)
    xs = x_sample.reshape(batch_s * seq_s, d_model)
    cond = jnp.concatenate([c_ctx[None], c, jnp.zeros((8 - 1 - batch_s, d_model), F32)], axis=0)

    pos_p = jnp.tile(jnp.arange(seq_p), ROW_TILE // seq_p)
    ret_tabs_p = _rope_tables(pos_p, 128)
    ret_tabs_s = _rope_tables(past + jnp.arange(seq_s), 128)
    t = jnp.arange(seq_s)
    row_t, col_t = _rope_tables(t // GRID_W, 64), _rope_tables(t % GRID_W, 64)
    att_tabs_s = tuple(jnp.concatenate([a, b], axis=1) for a, b in zip(row_t, col_t))

    tiles_per_seq = seq_s // ROW_TILE

    def mod_row_p(i):
        return 0

    def mod_row_s(i):
        return 1 + i // tiles_per_seq

    ck_bf, cv_bf = cache_k.astype(BF16), cache_v.astype(BF16)
    new_k, new_v, new_sf, new_sb = [], [], [], []
    for l in range(depth):
        last = l == depth - 1
        mod3 = _modulation(cond, w_mod[l], b_mod[l]).reshape(8, 1, 6 * d_model)
        w_in_b, wa_b, wr_b, wo_b = (w[l].astype(BF16) for w in (w_in, w_att_o, w_ret_o, w_out))
        wu_b, wd_b = w_up[l].astype(BF16), w_down[l].astype(BF16)
        decays = tuple(jnp.stack([jnp.repeat(decay_fwd[l], r), jnp.repeat(decay_bwd[l], r)])
                       for r in (r_dk, r_dv, RET_CHUNK))

        def layer(x2d, mod_row, ret_tabs, att_tabs, ctx, init, batch, seq_len, is_prompt):
            outs = _in_proj(x2d, mod3, mod_row, norm1_g[l], w_in_b, q_norm_g[l], k_norm_g[l], ret_tabs,
                            att_tabs, dims=dims, seq_len=seq_len, emit_cache=is_prompt)
            q, k, v, rq, rk, rv, rg, ga, gr = outs[:9]
            ya = _attention(q, k, v, ctx, l, batch=batch, seq_len=seq_len, n_kv=n_kv, group=group, hd=hd)
            ret = _retention(rq, rk, rv, rg, decays, init, l, batch=batch, seq_len=seq_len,
                             r_heads=r_heads, r_dk=r_dk, r_dv=r_dv, emit_state=is_prompt)
            x1, h2 = _merge(ya, ret[0], ga, gr, x2d, mod3, mod_row, norm2_g[l], wa_b, wr_b, wo_b)
            x2 = _conv_ffn(h2, x1, mod3, mod_row, wu_b, conv_w[l], conv_b[l], wd_b, final_g,
                           seq_len=seq_len, final_norm=last)
            return x2, outs[9:], ret[1:]

        xp, cache_new, state_new = layer(xp, mod_row_p, ret_tabs_p, None, None, None, batch_p, seq_p, True)
        new_k.append(cache_new[0])
        new_v.append(cache_new[1])
        new_sf.append(state_new[0])
        new_sb.append(state_new[1])
        xs, _, _ = layer(xs, mod_row_s, ret_tabs_s, att_tabs_s, (ck_bf, cv_bf),
                         (state_ret_fwd, state_ret_bwd), batch_s, seq_s, False)

    def stack(parts):
        return parts[0] if depth == 1 else jnp.concatenate(parts, axis=1)

    return (xp.reshape(batch_p, seq_p, d_model), xs.reshape(batch_s, seq_s, d_model),
            stack(new_k), stack(new_v), stack(new_sf), stack(new_sb))
```

```python
import functools

import jax
import jax.numpy as jnp
import numpy as np
from jax import lax
from jax.experimental import pallas as pl
from jax.experimental.pallas import tpu as pltpu

GRID_W = 64
ROPE_THETA = 10000.0
EPS = 1e-6
RET_CHUNK = 256
ROW_TILE = 512
Q_TILE = 256
KV_TILE = 2048
LOG2_E = float(np.log2(np.e))
SAFE_DENOMINATOR = 2.0 ** -80
HALO = 16
VMEM_LIMIT = 56 * 1024 * 1024

F32 = jnp.float32
BF16 = jnp.bfloat16


def _params(n_axes):
    return pltpu.CompilerParams(dimension_semantics=("arbitrary",) * n_axes,
                                vmem_limit_bytes=VMEM_LIMIT)


def _resident(shape):
    nd = len(shape)
    return pl.BlockSpec(shape, lambda *_: (0,) * nd, pipeline_mode=pl.Buffered(1))


def _rms(x):
    return x * lax.rsqrt(jnp.mean(x * x, axis=-1, keepdims=True) + EPS)


def _rotate(t, cos, sin_lo, sin_hi):
    return t * cos + pltpu.roll(t, 96, 1) * sin_lo + pltpu.roll(t, 32, 1) * sin_hi


def _rope_tables(pos, width):
    half = 32
    inv = ROPE_THETA ** (-jnp.arange(half, dtype=F32) / half)
    ang = pos.astype(F32)[:, None] * inv[None, :]
    cos, sin = jnp.cos(ang), jnp.sin(ang)
    zero = jnp.zeros_like(sin)
    reps = width // 64
    cos_t = jnp.tile(jnp.concatenate([cos, cos], axis=1), (1, reps))
    sin_lo = jnp.tile(jnp.concatenate([-sin, zero], axis=1), (1, reps))
    sin_hi = jnp.tile(jnp.concatenate([zero, sin], axis=1), (1, reps))
    return cos_t, sin_lo, sin_hi


def _mod_kernel(cond_ref, w_ref, b_ref, o_ref):
    cnd = cond_ref[...]
    s = (cnd * jax.nn.sigmoid(cnd)).astype(BF16)
    o_ref[...] = jnp.dot(s, w_ref[...].astype(BF16), preferred_element_type=F32) + b_ref[...]


def _modulation(cond, w_mod, b_mod):
    rows, d = cond.shape
    n = w_mod.shape[1]
    tn = 1024
    return pl.pallas_call(
        _mod_kernel,
        grid=(n // tn,),
        in_specs=[pl.BlockSpec((rows, d), lambda j: (0, 0)),
                  pl.BlockSpec((d, tn), lambda j: (0, j)),
                  pl.BlockSpec((1, tn), lambda j: (0, j))],
        out_specs=pl.BlockSpec((rows, tn), lambda j: (0, j)),
        out_shape=jax.ShapeDtypeStruct((rows, n), F32),
        compiler_params=_params(1),
        name="modulation",
    )(cond, w_mod, b_mod.reshape(1, n))


def _in_proj_kernel(*refs, dims, rope_qk, emit_cache, seq_len):
    n_heads, n_kv, hd, r_heads, r_dk, r_dv, d_model = dims
    it = iter(refs)
    x_ref, sh_ref, sc_ref, g_ref, w_ref, qg_ref, kg_ref = (next(it) for _ in range(7))
    rcos_ref, rlo_ref, rhi_ref = (next(it) for _ in range(3))
    if rope_qk:
        acos_ref, alo_ref, ahi_ref = (next(it) for _ in range(3))
    q_ref, k_ref, v_ref, rq_ref, rk_ref, rv_ref, rg_ref, ga_ref, gr_ref = (next(it) for _ in range(9))
    if emit_cache:
        ck_ref, cv_ref = next(it), next(it)

    x = x_ref[...]
    h = _rms(x) * g_ref[...]
    h = (h * (1.0 + sc_ref[0]) + sh_ref[0]).astype(BF16)

    col = [0]

    def proj(width):
        lo = col[0]
        col[0] = lo + width
        return jnp.dot(h, w_ref[:, lo:lo + width], preferred_element_type=F32)

    tm = x.shape[0]
    zq = proj(n_heads * hd)
    for i in range(n_heads):
        t = _rms(zq[:, i * hd:(i + 1) * hd]) * qg_ref[...]
        if rope_qk:
            t = _rotate(t, acos_ref[...], alo_ref[...], ahi_ref[...])
        q_ref[:, i * hd:(i + 1) * hd] = (t * (hd ** -0.5 * LOG2_E)).astype(BF16)
    zk = proj(n_kv * hd)
    zv = proj(n_kv * hd)
    v_ref[...] = zv.astype(BF16)
    for i in range(n_kv):
        t = _rms(zk[:, i * hd:(i + 1) * hd]) * kg_ref[...]
        if emit_cache:
            for s in range(tm // seq_len):
                ck_ref[s, 0, i] = t[s * seq_len:(s + 1) * seq_len]
                cv_ref[s, 0, i] = zv[s * seq_len:(s + 1) * seq_len, i * hd:(i + 1) * hd]
        if rope_qk:
            t = _rotate(t, acos_ref[...], alo_ref[...], ahi_ref[...])
        k_ref[:, i * hd:(i + 1) * hd] = t.astype(BF16)
    zrq = proj(r_heads * r_dk)
    zrk = proj(r_heads * r_dk)
    for i in range(r_heads * r_dk // 128):
        sl = slice(i * 128, (i + 1) * 128)
        rq_ref[:, sl] = _rotate(zrq[:, sl], rcos_ref[...], rlo_ref[...], rhi_ref[...]).astype(BF16)
        t = _rotate(zrk[:, sl], rcos_ref[...], rlo_ref[...], rhi_ref[...])
        rk_ref[:, sl] = (t * (r_dk ** -0.5)).astype(BF16)
    rv_ref[...] = proj(r_heads * r_dv).astype(BF16)
    zg = proj(r_heads * r_dv)
    rg_ref[...] = (zg * jax.nn.sigmoid(zg)).astype(BF16)
    ga_ref[...] = jax.nn.sigmoid(proj(d_model)).astype(BF16)
    gr_ref[...] = jax.nn.sigmoid(proj(d_model)).astype(BF16)


def _in_proj(x2d, mod3, mod_row, norm_g, w_in, q_g, k_g, ret_tabs, att_tabs, *, dims, seq_len,
             emit_cache):
    n_heads, n_kv, hd, r_heads, r_dk, r_dv, d_model = dims
    t_rows = x2d.shape[0]
    tm = ROW_TILE
    rope_qk = att_tabs is not None
    tab_blocks = ret_tabs[0].shape[0] // tm

    def row(i):
        return (i, 0)

    def tab(i):
        return (i % tab_blocks, 0)

    in_specs = [pl.BlockSpec((tm, d_model), row),
                pl.BlockSpec((1, 1, d_model), lambda i: (mod_row(i), 0, 0)),
                pl.BlockSpec((1, 1, d_model), lambda i: (mod_row(i), 0, 1)),
                _resident((1, d_model)),
                _resident(w_in.shape),
                _resident((1, hd)), _resident((1, hd))]
    args = [x2d, mod3, mod3, norm_g.reshape(1, d_model), w_in, q_g.reshape(1, hd), k_g.reshape(1, hd)]
    for tb in ret_tabs:
        in_specs.append(pl.BlockSpec((tm, 128), tab))
        args.append(tb)
    if rope_qk:
        for tb in att_tabs:
            in_specs.append(pl.BlockSpec((tm, hd), tab))
            args.append(tb)
    widths = [n_heads * hd, n_kv * hd, n_kv * hd, r_heads * r_dk, r_heads * r_dk,
              r_heads * r_dv, r_heads * r_dv, d_model, d_model]
    out_specs = [pl.BlockSpec((tm, w), row) for w in widths]
    out_shape = [jax.ShapeDtypeStruct((t_rows, w), BF16) for w in widths]
    if emit_cache:
        nb = t_rows // seq_len
        for _ in range(2):
            out_specs.append(pl.BlockSpec((tm // seq_len, 1, n_kv, seq_len, hd),
                                          lambda i: (i, 0, 0, 0, 0)))
            out_shape.append(jax.ShapeDtypeStruct((nb, 1, n_kv, seq_len, hd), F32))
    return pl.pallas_call(
        functools.partial(_in_proj_kernel, dims=dims, rope_qk=rope_qk, emit_cache=emit_cache,
                          seq_len=seq_len),
        grid=(t_rows // tm,),
        in_specs=in_specs, out_specs=out_specs, out_shape=out_shape,
        compiler_params=_params(1),
        name="in_proj",
    )(*args)


def _attention_kernel(*refs, group, hd, seq_len, has_ctx):
    if has_ctx:
        q_ref, k_ref, v_ref, ck_ref, cv_ref, o_ref, vt_scr, k2_scr = refs
    else:
        q_ref, k_ref, v_ref, o_ref, vt_scr, k2_scr = refs
    tq = q_ref.shape[0]
    tk = min(KV_TILE, seq_len)
    chunks = [(k_ref, c * tk, c * tk, tk) for c in range(seq_len // tk)]
    if has_ctx:
        chunks.append((None, 0, seq_len, ck_ref.shape[3]))

    def keys(ref, row, width):
        return ck_ref[0, 0, 0] if ref is None else ref[row:row + width, :]

    @pl.when(pl.program_id(2) == 0)
    def _():
        k2 = None
        for ref, row, off, width in chunks:
            vals = cv_ref[0, 0, 0] if ref is None else v_ref[row:row + width, :]
            vt_scr[:hd, off:off + width] = vals.astype(F32).T.astype(BF16)
            kf = keys(ref, row, width).astype(F32)
            k2_c = jnp.max(jnp.sum(kf * kf, axis=1, keepdims=True), axis=0, keepdims=True)
            k2 = k2_c if k2 is None else jnp.maximum(k2, k2_c)
        vt_scr[hd:, :] = jnp.ones((vt_scr.shape[0] - hd, vt_scr.shape[1]), BF16)
        k2_scr[...] = jnp.broadcast_to(k2, k2_scr.shape)

    qt = jnp.concatenate([q_ref[:, g * hd:(g + 1) * hd].astype(F32).T.astype(BF16) for g in range(group)],
                         axis=1)

    def write(acc):
        out = acc[:hd] / acc[hd:hd + 1]
        for g in range(group):
            o_ref[:, g * hd:(g + 1) * hd] = out[:, g * tq:(g + 1) * tq].T.astype(BF16)

    qf = qt.astype(F32)
    bound = jnp.sqrt(jnp.sum(qf * qf, axis=0, keepdims=True) * k2_scr[0:1, 0:1])
    acc = None
    for ref, row, off, width in chunks:
        st = jnp.dot(keys(ref, row, width), qt, preferred_element_type=F32)
        p = jnp.exp2(st - bound).astype(BF16)
        pv = jnp.dot(vt_scr[:, off:off + width], p, preferred_element_type=F32)
        acc = pv if acc is None else acc + pv
    safe = jnp.min(acc[hd:hd + 1]) >= SAFE_DENOMINATOR

    @pl.when(safe)
    def _():
        write(acc)

    @pl.when(jnp.logical_not(safe))
    def _():
        m = acc_x = None
        for ref, row, off, width in chunks:
            st = jnp.dot(keys(ref, row, width), qt, preferred_element_type=F32)
            m_c = jnp.max(st, axis=0, keepdims=True)
            m_new = m_c if m is None else jnp.maximum(m, m_c)
            p = jnp.exp2(st - m_new).astype(BF16)
            pv = jnp.dot(vt_scr[:, off:off + width], p, preferred_element_type=F32)
            acc_x = pv if m is None else jnp.exp2(m - m_new) * acc_x + pv
            m = m_new
        write(acc_x)


def _attention(q, k, v, ctx, layer, *, batch, seq_len, n_kv, group, hd):
    t_rows = q.shape[0]
    tq = min(Q_TILE, seq_len)
    nq = seq_len // tq
    has_ctx = ctx is not None
    in_specs = [pl.BlockSpec((tq, group * hd), lambda b, h, i: (b * nq + i, h)),
                pl.BlockSpec((seq_len, hd), lambda b, h, i: (b, h)),
                pl.BlockSpec((seq_len, hd), lambda b, h, i: (b, h))]
    args = [q, k, v]
    n_keys = seq_len
    if has_ctx:
        past = ctx[0].shape[3]
        n_keys += past
        for a in ctx:
            in_specs.append(pl.BlockSpec((1, 1, 1, past, hd), lambda b, h, i: (b, layer, h, 0, 0)))
            args.append(a)
    return pl.pallas_call(
        functools.partial(_attention_kernel, group=group, hd=hd, seq_len=seq_len, has_ctx=has_ctx),
        grid=(batch, n_kv, nq),
        in_specs=in_specs,
        out_specs=pl.BlockSpec((tq, group * hd), lambda b, h, i: (b * nq + i, h)),
        out_shape=jax.ShapeDtypeStruct((t_rows, n_kv * group * hd), BF16),
        scratch_shapes=[pltpu.VMEM((hd + HALO, n_keys), BF16), pltpu.VMEM((8, 128), F32)],
        compiler_params=_params(3),
        name="attention",
    )(*args)


def _log_sigmoid(x):
    return jnp.minimum(x, 0.0) - jnp.log1p(jnp.exp(-jnp.abs(x)))


def _retention_kernel(*refs, r_dk, r_dv, seq_len, has_init, emit_state):
    it = iter(refs)
    rq_ref, rk_ref, rv_ref, rg_ref, dk_ref, dv_ref, dc_ref = (next(it) for _ in range(7))
    if has_init:
        s0f_ref, s0b_ref = next(it), next(it)
    o_ref = next(it)
    if emit_state:
        sf_ref, sb_ref = next(it), next(it)
    sprev_scr = next(it)

    c = RET_CHUNK
    n_chunks = seq_len // c
    kw, vw = 2 * r_dk, 2 * r_dv

    lg_k = _log_sigmoid(dk_ref[...])
    lg_v = _log_sigmoid(dv_ref[...])
    lg_c = _log_sigmoid(dc_ref[...])
    idx = lax.broadcasted_iota(jnp.int32, (c, 1), 0).astype(F32)
    q_dec_f = jnp.exp(lg_k[0:1] * (idx + 1.0))
    q_dec_b = jnp.exp(lg_k[1:2] * (c - idx))
    k_dec_f = jnp.exp(lg_k[0:1] * (c - 1.0 - idx))
    k_dec_b = jnp.exp(lg_k[1:2] * idx)
    chunk_f = jnp.exp(lg_v[0:1] * float(c))
    chunk_b = jnp.exp(lg_v[1:2] * float(c))
    diff = (lax.broadcasted_iota(jnp.int32, (c, c), 0)
            - lax.broadcasted_iota(jnp.int32, (c, c), 1)).astype(F32)

    def both_ways(head):
        lf = lg_c[0:1, head * c:(head + 1) * c]
        lb = lg_c[1:2, head * c:(head + 1) * c]
        fwd = jnp.where(diff >= 0, jnp.exp(lf * jnp.maximum(diff, 0.0)), 0.0)
        bwd = jnp.where(diff <= 0, jnp.exp(lb * jnp.maximum(-diff, 0.0)), 0.0)
        return fwd + bwd

    dmat = [both_ways(0), both_ways(1)]
    same_head = ((lax.broadcasted_iota(jnp.int32, (kw, vw), 0) // r_dk)
                 == (lax.broadcasted_iota(jnp.int32, (kw, vw), 1) // r_dv))
    k_lane = lax.broadcasted_iota(jnp.int32, (c, kw), 1)

    def load_state(ref):
        top = jnp.concatenate([ref[0, 0, 0], jnp.zeros((r_dk, r_dv), F32)], axis=1)
        bot = jnp.concatenate([jnp.zeros((r_dk, r_dv), F32), ref[0, 0, 1]], axis=1)
        return jnp.concatenate([top, bot], axis=0)

    def store_state(ref, s):
        ref[0, 0, 0] = s[:r_dk, :r_dv]
        ref[0, 0, 1] = s[r_dk:, r_dv:]

    def outer_kv(kd, v):
        kv = lax.dot_general(kd.astype(BF16), v, (((0,), (0,)), ((), ())), preferred_element_type=F32)
        return jnp.where(same_head, kv, 0.0)

    def rows(n):
        return pl.ds(pl.multiple_of(n * c, c), c)

    def fwd_body(n, s):
        sprev_scr[n] = s
        k = rk_ref[rows(n), :].astype(F32)
        return chunk_f * s + outer_kv(k * k_dec_f, rv_ref[rows(n), :])

    s_f = load_state(s0f_ref) if has_init else jnp.zeros((kw, vw), F32)
    s_f = lax.fori_loop(0, n_chunks, fwd_body, s_f, unroll=min(8, n_chunks))
    if emit_state:
        store_state(sf_ref, s_f)

    def bwd_body(j, s_b):
        n = n_chunks - 1 - j
        q = rq_ref[rows(n), :]
        k = rk_ref[rows(n), :]
        v = rv_ref[rows(n), :]
        zero = jnp.zeros_like(q)
        q2 = jnp.concatenate([jnp.where(k_lane < r_dk, q, zero), jnp.where(k_lane >= r_dk, q, zero)], axis=0)
        sc = lax.dot_general(q2, k, (((1,), (1,)), ((), ())), preferred_element_type=F32)
        intra = jnp.concatenate(
            [jnp.dot((sc[h * c:(h + 1) * c] * dmat[h]).astype(BF16), v[:, h * r_dv:(h + 1) * r_dv],
                     preferred_element_type=F32) for h in range(2)], axis=1)
        qf = q.astype(F32)
        q_both = jnp.concatenate([(qf * q_dec_f).astype(BF16), (qf * q_dec_b).astype(BF16)], axis=1)
        s_both = jnp.concatenate([sprev_scr[n], s_b], axis=0).astype(BF16)
        o = intra + jnp.dot(q_both, s_both, preferred_element_type=F32)
        o = jnp.concatenate([_rms(o[:, h * r_dv:(h + 1) * r_dv]) for h in range(2)], axis=1)
        o_ref[rows(n), :] = (rg_ref[rows(n), :].astype(F32) * o).astype(BF16)
        return chunk_b * s_b + outer_kv(k.astype(F32) * k_dec_b, v)

    s_b = load_state(s0b_ref) if has_init else jnp.zeros((kw, vw), F32)
    s_b = lax.fori_loop(0, n_chunks, bwd_body, s_b, unroll=min(4, n_chunks))
    if emit_state:
        store_state(sb_ref, s_b)


def _retention(rq, rk, rv, rg, decays, init, layer, *, batch, seq_len, r_heads, r_dk, r_dv, emit_state):
    t_rows = rq.shape[0]
    pairs = r_heads // 2
    kw, vw = 2 * r_dk, 2 * r_dv
    has_init = init is not None
    in_specs = [pl.BlockSpec((seq_len, kw), lambda b, p: (b, p)),
                pl.BlockSpec((seq_len, kw), lambda b, p: (b, p)),
                pl.BlockSpec((seq_len, vw), lambda b, p: (b, p)),
                pl.BlockSpec((seq_len, vw), lambda b, p: (b, p)),
                pl.BlockSpec((2, kw), lambda b, p: (0, p)),
                pl.BlockSpec((2, vw), lambda b, p: (0, p)),
                pl.BlockSpec((2, 2 * RET_CHUNK), lambda b, p: (0, p))]
    args = [rq, rk, rv, rg, *decays]
    state_spec = pl.BlockSpec((1, 1, 2, r_dk, r_dv), lambda b, p: (b, layer if has_init else 0, p, 0, 0))
    if has_init:
        in_specs += [state_spec, state_spec]
        args += list(init)
    out_specs = [pl.BlockSpec((seq_len, vw), lambda b, p: (b, p))]
    out_shape = [jax.ShapeDtypeStruct((t_rows, r_heads * r_dv), BF16)]
    if emit_state:
        new_spec = pl.BlockSpec((1, 1, 2, r_dk, r_dv), lambda b, p: (b, 0, p, 0, 0))
        out_specs += [new_spec, new_spec]
        out_shape += [jax.ShapeDtypeStruct((batch, 1, r_heads, r_dk, r_dv), F32)] * 2
    return pl.pallas_call(
        functools.partial(_retention_kernel, r_dk=r_dk, r_dv=r_dv, seq_len=seq_len, has_init=has_init,
                          emit_state=emit_state),
        grid=(batch, pairs),
        in_specs=in_specs, out_specs=out_specs, out_shape=out_shape,
        scratch_shapes=[pltpu.VMEM((seq_len // RET_CHUNK, kw, vw), F32)],
        compiler_params=_params(2),
        name="retention",
    )(*args)


def _merge_kernel(ya_ref, yr_ref, ga_ref, gr_ref, x_ref, g1_ref, sh_ref, sc_ref, n2_ref,
                  wa_ref, wr_ref, wo_ref, x1_ref, h2_ref):
    y = (ga_ref[...].astype(F32) * jnp.dot(ya_ref[...], wa_ref[...], preferred_element_type=F32)
         + gr_ref[...].astype(F32) * jnp.dot(yr_ref[...], wr_ref[...], preferred_element_type=F32))
    x1 = x_ref[...] + g1_ref[0] * jnp.dot(y.astype(BF16), wo_ref[...], preferred_element_type=F32)
    x1_ref[...] = x1
    h2 = _rms(x1) * n2_ref[...]
    h2_ref[...] = (h2 * (1.0 + sc_ref[0]) + sh_ref[0]).astype(BF16)


def _merge(ya, yr, ga, gr, x2d, mod3, mod_row, norm2_g, wa, wr, wo):
    t_rows, d = x2d.shape
    tm = ROW_TILE

    def row(i):
        return (i, 0)

    def mod(j):
        return pl.BlockSpec((1, 1, d), lambda i: (mod_row(i), 0, j))

    act = pl.BlockSpec((tm, d), row)
    return pl.pallas_call(
        _merge_kernel,
        grid=(t_rows // tm,),
        in_specs=[act, act, act, act, act, mod(2), mod(3), mod(4), _resident((1, d)),
                  _resident(wa.shape), _resident(wr.shape), _resident(wo.shape)],
        out_specs=[act, act],
        out_shape=[jax.ShapeDtypeStruct((t_rows, d), F32), jax.ShapeDtypeStruct((t_rows, d), BF16)],
        compiler_params=_params(1),
        name="merge",
    )(ya, yr, ga, gr, x2d, mod3, mod3, mod3, norm2_g.reshape(1, d), wa, wr, wo)


def _ffn_kernel(h_ref, hp_ref, hn_ref, x_ref, g2_ref, wu_ref, cw_ref, cb_ref, wd_ref, fg_ref,
                o_ref, u_scr, *, d_ff, seq_len, final_norm):
    tm = h_ref.shape[0]
    tile = pl.program_id(0) % (seq_len // tm)
    hp = jnp.where(tile == 0, jnp.zeros_like(hp_ref), hp_ref[...])
    hn = jnp.where(tile == seq_len // tm - 1, jnp.zeros_like(hn_ref), hn_ref[...])
    h_ext = jnp.concatenate([hp, h_ref[...], hn], axis=0)
    n_split = 2
    ck = d_ff // n_split
    acc = None
    for j in range(n_split):
        halves = []
        for base in (j * ck, d_ff + j * ck):
            u = jnp.dot(h_ext, wu_ref[:, base:base + ck], preferred_element_type=F32)
            rows = u.shape[0]
            prev = pltpu.roll(u, 1, 0)[HALO:HALO + tm]
            here = u[HALO:HALO + tm]
            nxt = pltpu.roll(u, rows - 1, 0)[HALO:HALO + tm]
            cols = slice(base, base + ck)
            halves.append(cw_ref[0:1, cols] * prev + cw_ref[1:2, cols] * here
                          + cw_ref[2:3, cols] * nxt + cb_ref[:, cols])
        a, g = halves
        act = (g * jax.nn.sigmoid(g) * a).astype(BF16)
        part = jnp.dot(act, wd_ref[j * ck:(j + 1) * ck, :], preferred_element_type=F32)
        acc = part if acc is None else acc + part
    x2 = x_ref[...] + g2_ref[0] * acc
    if final_norm:
        x2 = _rms(x2) * fg_ref[...]
    o_ref[...] = x2


def _conv_ffn(h2, x1, mod3, mod_row, w_up, conv_w, conv_b, w_down, final_g, *, seq_len, final_norm):
    t_rows, d = x1.shape
    d_ff = w_down.shape[0]
    tm = min(ROW_TILE, seq_len)
    assert seq_len % tm == 0
    per = tm // HALO
    last = t_rows // HALO - 1

    def row(i):
        return (i, 0)

    return pl.pallas_call(
        functools.partial(_ffn_kernel, d_ff=d_ff, seq_len=seq_len, final_norm=final_norm),
        grid=(t_rows // tm,),
        in_specs=[pl.BlockSpec((tm, d), row),
                  pl.BlockSpec((HALO, d), lambda i: (jnp.maximum(i * per - 1, 0), 0)),
                  pl.BlockSpec((HALO, d), lambda i: (jnp.minimum((i + 1) * per, last), 0)),
                  pl.BlockSpec((tm, d), row),
                  pl.BlockSpec((1, 1, d), lambda i: (mod_row(i), 0, 5)),
                  _resident(w_up.shape), _resident(conv_w.shape), _resident((1, 2 * d_ff)),
                  _resident(w_down.shape), _resident((1, d))],
        out_specs=pl.BlockSpec((tm, d), row),
        out_shape=jax.ShapeDtypeStruct((t_rows, d), F32),
        scratch_shapes=[pltpu.VMEM((tm + 2 * HALO, d_ff // 2), F32)],
        compiler_params=_params(1),
        name="conv_ffn",
    )(h2, h2, h2, x1, mod3, w_up, conv_w, conv_b.reshape(1, 2 * d_ff), w_down, final_g.reshape(1, d))


def kernel(x_prompt, x_sample, c, cache_k, cache_v, state_ret_fwd, state_ret_bwd, c_ctx, norm1_g, norm2_g, w_mod, b_mod, w_in, q_norm_g, k_norm_g, decay_fwd, decay_bwd, w_att_o, w_ret_o, w_out, w_up, conv_w, conv_b, w_down, final_g):
    batch_p, seq_p, d_model = x_prompt.shape
    batch_s, seq_s, _ = x_sample.shape
    depth = w_in.shape[0]
    n_kv, past, hd = cache_k.shape[2], cache_k.shape[3], cache_k.shape[4]
    r_heads, r_dk, r_dv = state_ret_fwd.shape[2], state_ret_fwd.shape[3], state_ret_fwd.shape[4]
    n_heads = w_att_o.shape[1] // hd
    group = n_heads // n_kv
    dims = (n_heads, n_kv, hd, r_heads, r_dk, r_dv, d_model)
    assert seq_s % GRID_W == 0 and seq_s % ROW_TILE == 0 and ROW_TILE % seq_p == 0
    assert hd == 128 and 2 * r_dk == 128 and r_heads % 2 == 0
    assert seq_p % RET_CHUNK == 0 and seq_s % RET_CHUNK == 0

    xp = x_prompt.reshape(batch_p * seq_p, d_model)
    xs = x_sample.reshape(batch_s * seq_s, d_model)
    cond = jnp.concatenate([c_ctx[None], c, jnp.zeros((8 - 1 - batch_s, d_model), F32)], axis=0)

    pos_p = jnp.tile(jnp.arange(seq_p), ROW_TILE // seq_p)
    ret_tabs_p = _rope_tables(pos_p, 128)
    ret_tabs_s = _rope_tables(past + jnp.arange(seq_s), 128)
    t = jnp.arange(seq_s)
    row_t, col_t = _rope_tables(t // GRID_W, 64), _rope_tables(t % GRID_W, 64)
    att_tabs_s = tuple(jnp.concatenate([a, b], axis=1) for a, b in zip(row_t, col_t))

    tiles_per_seq = seq_s // ROW_TILE

    def mod_row_p(i):
        return 0

    def mod_row_s(i):
        return 1 + i // tiles_per_seq

    ck_bf, cv_bf = cache_k.astype(BF16), cache_v.astype(BF16)
    new_k, new_v, new_sf, new_sb = [], [], [], []
    for l in range(depth):
        last = l == depth - 1
        mod3 = _modulation(cond, w_mod[l], b_mod[l]).reshape(8, 1, 6 * d_model)
        w_in_b, wa_b, wr_b, wo_b = (w[l].astype(BF16) for w in (w_in, w_att_o, w_ret_o, w_out))
        wu_b, wd_b = w_up[l].astype(BF16), w_down[l].astype(BF16)
        decays = tuple(jnp.stack([jnp.repeat(decay_fwd[l], r), jnp.repeat(decay_bwd[l], r)])
                       for r in (r_dk, r_dv, RET_CHUNK))

        def layer(x2d, mod_row, ret_tabs, att_tabs, ctx, init, batch, seq_len, is_prompt):
            outs = _in_proj(x2d, mod3, mod_row, norm1_g[l], w_in_b, q_norm_g[l], k_norm_g[l], ret_tabs,
                            att_tabs, dims=dims, seq_len=seq_len, emit_cache=is_prompt)
            q, k, v, rq, rk, rv, rg, ga, gr = outs[:9]
            ya = _attention(q, k, v, ctx, l, batch=batch, seq_len=seq_len, n_kv=n_kv, group=group, hd=hd)
            ret = _retention(rq, rk, rv, rg, decays, init, l, batch=batch, seq_len=seq_len,
                             r_heads=r_heads, r_dk=r_dk, r_dv=r_dv, emit_state=is_prompt)
            x1, h2 = _merge(ya, ret[0], ga, gr, x2d, mod3, mod_row, norm2_g[l], wa_b, wr_b, wo_b)
            x2 = _conv_ffn(h2, x1, mod3, mod_row, wu_b, conv_w[l], conv_b[l], wd_b, final_g,
                           seq_len=seq_len, final_norm=last)
            return x2, outs[9:], ret[1:]

        xp, cache_new, state_new = layer(xp, mod_row_p, ret_tabs_p, None, None, None, batch_p, seq_p, True)
        new_k.append(cache_new[0])
        new_v.append(cache_new[1])
        new_sf.append(state_new[0])
        new_sb.append(state_new[1])
        xs, _, _ = layer(xs, mod_row_s, ret_tabs_s, att_tabs_s, (ck_bf, cv_bf),
                         (state_ret_fwd, state_ret_bwd), batch_s, seq_s, False)

    def stack(parts):
        return parts[0] if depth == 1 else jnp.concatenate(parts, axis=1)

    return (xp.reshape(batch_p, seq_p, d_model), xs.reshape(batch_s, seq_s, d_model),
            stack(new_k), stack(new_v), stack(new_sf), stack(new_sb))
```

```python
import functools

import jax
import jax.numpy as jnp
import numpy as np
from jax import lax
from jax.experimental import pallas as pl
from jax.experimental.pallas import tpu as pltpu

GRID_W = 64
ROPE_THETA = 10000.0
EPS = 1e-6
RET_CHUNK = 256
ROW_TILE = 512
Q_TILE = 256
KV_TILE = 2048
LOG2_E = float(np.log2(np.e))
SAFE_DENOMINATOR = 2.0 ** -80
HALO = 16
VMEM_LIMIT = 56 * 1024 * 1024

F32 = jnp.float32
BF16 = jnp.bfloat16


def _params(n_axes):
    return pltpu.CompilerParams(dimension_semantics=("arbitrary",) * n_axes,
                                vmem_limit_bytes=VMEM_LIMIT)


def _resident(shape):
    nd = len(shape)
    return pl.BlockSpec(shape, lambda *_: (0,) * nd, pipeline_mode=pl.Buffered(1))


def _rms(x):
    return x * lax.rsqrt(jnp.mean(x * x, axis=-1, keepdims=True) + EPS)


def _rotate(t, cos, sin_signed):
    lane = lax.broadcasted_iota(jnp.int32, t.shape, 1)
    partner = jnp.where(lane % 64 < 32, pltpu.roll(t, 96, 1), pltpu.roll(t, 32, 1))
    return t * cos + partner * sin_signed


def _rope_tables(pos, width):
    half = 32
    inv = np.float32(ROPE_THETA) ** (-np.arange(half, dtype=np.float32) / np.float32(half))
    ang = np.asarray(pos, np.float32)[:, None] * inv[None, :]
    cos, sin = np.cos(ang), np.sin(ang)
    reps = width // 64
    return (np.tile(np.concatenate([cos, cos], axis=1), (1, reps)),
            np.tile(np.concatenate([-sin, sin], axis=1), (1, reps)))


def _mod_kernel(cond_ref, w_ref, b_ref, o_ref):
    cnd = cond_ref[...]
    s = (cnd * jax.nn.sigmoid(cnd)).astype(BF16)
    o_ref[...] = jnp.dot(s, w_ref[...].astype(BF16), preferred_element_type=F32) + b_ref[...]


def _modulation(cond, w_mod, b_mod):
    rows, d = cond.shape
    n = w_mod.shape[1]
    tn = 1024
    return pl.pallas_call(
        _mod_kernel,
        grid=(n // tn,),
        in_specs=[pl.BlockSpec((rows, d), lambda j: (0, 0)),
                  pl.BlockSpec((d, tn), lambda j: (0, j)),
                  pl.BlockSpec((1, tn), lambda j: (0, j))],
        out_specs=pl.BlockSpec((rows, tn), lambda j: (0, j)),
        out_shape=jax.ShapeDtypeStruct((rows, n), F32),
        compiler_params=_params(1),
        name="modulation",
    )(cond, w_mod, b_mod.reshape(1, n))


def _in_proj_kernel(*refs, dims, rope_qk, emit_cache, seq_len):
    n_heads, n_kv, hd, r_heads, r_dk, r_dv, d_model = dims
    it = iter(refs)
    x_ref, sh_ref, sc_ref, g_ref, w_ref, qg_ref, kg_ref = (next(it) for _ in range(7))
    rcos_ref, rsin_ref = next(it), next(it)
    if rope_qk:
        acos_ref, asin_ref = next(it), next(it)
    q_ref, k_ref, v_ref, rq_ref, rk_ref, rv_ref, rg_ref, ga_ref, gr_ref = (next(it) for _ in range(9))
    if emit_cache:
        ck_ref, cv_ref = next(it), next(it)

    x = x_ref[...]
    h = _rms(x) * g_ref[...]
    h = (h * (1.0 + sc_ref[0]) + sh_ref[0]).astype(BF16)

    col = [0]

    def proj(width):
        lo = col[0]
        col[0] = lo + width
        return jnp.dot(h, w_ref[:, lo:lo + width], preferred_element_type=F32)

    tm = x.shape[0]
    zq = proj(n_heads * hd)
    for i in range(n_heads):
        t = _rms(zq[:, i * hd:(i + 1) * hd]) * qg_ref[...]
        if rope_qk:
            t = _rotate(t, acos_ref[...], asin_ref[...])
        q_ref[:, i * hd:(i + 1) * hd] = (t * (hd ** -0.5 * LOG2_E)).astype(BF16)
    zk = proj(n_kv * hd)
    zv = proj(n_kv * hd)
    v_ref[...] = zv.astype(BF16)
    for i in range(n_kv):
        t = _rms(zk[:, i * hd:(i + 1) * hd]) * kg_ref[...]
        if emit_cache:
            for s in range(tm // seq_len):
                ck_ref[s, 0, i] = t[s * seq_len:(s + 1) * seq_len]
                cv_ref[s, 0, i] = zv[s * seq_len:(s + 1) * seq_len, i * hd:(i + 1) * hd]
        if rope_qk:
            t = _rotate(t, acos_ref[...], asin_ref[...])
        k_ref[:, i * hd:(i + 1) * hd] = t.astype(BF16)
    zrq = proj(r_heads * r_dk)
    zrk = proj(r_heads * r_dk)
    for i in range(r_heads * r_dk // 128):
        sl = slice(i * 128, (i + 1) * 128)
        rq_ref[:, sl] = _rotate(zrq[:, sl], rcos_ref[...], rsin_ref[...]).astype(BF16)
        t = _rotate(zrk[:, sl], rcos_ref[...], rsin_ref[...])
        rk_ref[:, sl] = (t * (r_dk ** -0.5)).astype(BF16)
    rv_ref[...] = proj(r_heads * r_dv).astype(BF16)
    zg = proj(r_heads * r_dv)
    rg_ref[...] = (zg * jax.nn.sigmoid(zg)).astype(BF16)
    ga_ref[...] = jax.nn.sigmoid(proj(d_model)).astype(BF16)
    gr_ref[...] = jax.nn.sigmoid(proj(d_model)).astype(BF16)


def _in_proj(x2d, mod3, mod_row, norm_g, w_in, q_g, k_g, ret_tabs, att_tabs, *, dims, seq_len,
             emit_cache):
    n_heads, n_kv, hd, r_heads, r_dk, r_dv, d_model = dims
    t_rows = x2d.shape[0]
    tm = ROW_TILE
    rope_qk = att_tabs is not None
    tab_blocks = ret_tabs[0].shape[0] // tm

    def row(i):
        return (i, 0)

    def tab(i):
        return (i % tab_blocks, 0)

    in_specs = [pl.BlockSpec((tm, d_model), row),
                pl.BlockSpec((1, 1, d_model), lambda i: (mod_row(i), 0, 0)),
                pl.BlockSpec((1, 1, d_model), lambda i: (mod_row(i), 0, 1)),
                _resident((1, d_model)),
                _resident(w_in.shape),
                _resident((1, hd)), _resident((1, hd))]
    args = [x2d, mod3, mod3, norm_g.reshape(1, d_model), w_in, q_g.reshape(1, hd), k_g.reshape(1, hd)]
    for tb in ret_tabs:
        in_specs.append(pl.BlockSpec((tm, 128), tab))
        args.append(tb)
    if rope_qk:
        for tb in att_tabs:
            in_specs.append(pl.BlockSpec((tm, hd), tab))
            args.append(tb)
    widths = [n_heads * hd, n_kv * hd, n_kv * hd, r_heads * r_dk, r_heads * r_dk,
              r_heads * r_dv, r_heads * r_dv, d_model, d_model]
    out_specs = [pl.BlockSpec((tm, w), row) for w in widths]
    out_shape = [jax.ShapeDtypeStruct((t_rows, w), BF16) for w in widths]
    if emit_cache:
        nb = t_rows // seq_len
        for _ in range(2):
            out_specs.append(pl.BlockSpec((tm // seq_len, 1, n_kv, seq_len, hd),
                                          lambda i: (i, 0, 0, 0, 0)))
            out_shape.append(jax.ShapeDtypeStruct((nb, 1, n_kv, seq_len, hd), F32))
    return pl.pallas_call(
        functools.partial(_in_proj_kernel, dims=dims, rope_qk=rope_qk, emit_cache=emit_cache,
                          seq_len=seq_len),
        grid=(t_rows // tm,),
        in_specs=in_specs, out_specs=out_specs, out_shape=out_shape,
        compiler_params=_params(1),
        name="in_proj",
    )(*args)


def _attention_kernel(*refs, group, hd, seq_len, has_ctx):
    if has_ctx:
        q_ref, k_ref, v_ref, ck_ref, cv_ref, o_ref, vt_scr, k2_scr = refs
    else:
        q_ref, k_ref, v_ref, o_ref, vt_scr, k2_scr = refs
    tq = q_ref.shape[0]
    tk = min(KV_TILE, seq_len)
    chunks = [(k_ref, c * tk, c * tk, tk) for c in range(seq_len // tk)]
    if has_ctx:
        chunks.append((None, 0, seq_len, ck_ref.shape[3]))

    def keys(ref, row, width):
        return ck_ref[0, 0, 0] if ref is None else ref[row:row + width, :]

    @pl.when(pl.program_id(2) == 0)
    def _():
        k2 = None
        for ref, row, off, width in chunks:
            vals = cv_ref[0, 0, 0] if ref is None else v_ref[row:row + width, :]
            vt_scr[:hd, off:off + width] = vals.astype(F32).T.astype(BF16)
            kf = keys(ref, row, width).astype(F32)
            k2_c = jnp.max(jnp.sum(kf * kf, axis=1, keepdims=True), axis=0, keepdims=True)
            k2 = k2_c if k2 is None else jnp.maximum(k2, k2_c)
        vt_scr[hd:, :] = jnp.ones((vt_scr.shape[0] - hd, vt_scr.shape[1]), BF16)
        k2_scr[...] = jnp.broadcast_to(k2, k2_scr.shape)

    qt = jnp.concatenate([q_ref[:, g * hd:(g + 1) * hd].astype(F32).T.astype(BF16) for g in range(group)],
                         axis=1)

    def write(acc):
        out = acc[:hd] / acc[hd:hd + 1]
        for g in range(group):
            o_ref[:, g * hd:(g + 1) * hd] = out[:, g * tq:(g + 1) * tq].T.astype(BF16)

    qf = qt.astype(F32)
    bound = jnp.sqrt(jnp.sum(qf * qf, axis=0, keepdims=True) * k2_scr[0:1, 0:1])
    acc = None
    for ref, row, off, width in chunks:
        st = jnp.dot(keys(ref, row, width), qt, preferred_element_type=F32)
        p = jnp.exp2(st - bound).astype(BF16)
        pv = jnp.dot(vt_scr[:, off:off + width], p, preferred_element_type=F32)
        acc = pv if acc is None else acc + pv
    safe = jnp.min(acc[hd:hd + 1]) >= SAFE_DENOMINATOR

    @pl.when(safe)
    def _():
        write(acc)

    @pl.when(jnp.logical_not(safe))
    def _():
        m = acc_x = None
        for ref, row, off, width in chunks:
            st = jnp.dot(keys(ref, row, width), qt, preferred_element_type=F32)
            m_c = jnp.max(st, axis=0, keepdims=True)
            m_new = m_c if m is None else jnp.maximum(m, m_c)
            p = jnp.exp2(st - m_new).astype(BF16)
            pv = jnp.dot(vt_scr[:, off:off + width], p, preferred_element_type=F32)
            acc_x = pv if m is None else jnp.exp2(m - m_new) * acc_x + pv
            m = m_new
        write(acc_x)


def _attention(q, k, v, ctx, layer, *, batch, seq_len, n_kv, group, hd):
    t_rows = q.shape[0]
    tq = min(Q_TILE, seq_len)
    nq = seq_len // tq
    has_ctx = ctx is not None
    in_specs = [pl.BlockSpec((tq, group * hd), lambda b, h, i: (b * nq + i, h)),
                pl.BlockSpec((seq_len, hd), lambda b, h, i: (b, h)),
                pl.BlockSpec((seq_len, hd), lambda b, h, i: (b, h))]
    args = [q, k, v]
    n_keys = seq_len
    if has_ctx:
        past = ctx[0].shape[3]
        n_keys += past
        for a in ctx:
            in_specs.append(pl.BlockSpec((1, 1, 1, past, hd), lambda b, h, i: (b, layer, h, 0, 0)))
            args.append(a)
    return pl.pallas_call(
        functools.partial(_attention_kernel, group=group, hd=hd, seq_len=seq_len, has_ctx=has_ctx),
        grid=(batch, n_kv, nq),
        in_specs=in_specs,
        out_specs=pl.BlockSpec((tq, group * hd), lambda b, h, i: (b * nq + i, h)),
        out_shape=jax.ShapeDtypeStruct((t_rows, n_kv * group * hd), BF16),
        scratch_shapes=[pltpu.VMEM((hd + HALO, n_keys), BF16), pltpu.VMEM((8, 128), F32)],
        compiler_params=_params(3),
        name="attention",
    )(*args)


def _log_sigmoid(x):
    return jnp.minimum(x, 0.0) - jnp.log1p(jnp.exp(-jnp.abs(x)))


def _retention_kernel(*refs, r_dk, r_dv, seq_len, has_init, emit_state):
    it = iter(refs)
    rq_ref, rk_ref, rv_ref, rg_ref, dk_ref, dv_ref, dc_ref = (next(it) for _ in range(7))
    if has_init:
        s0f_ref, s0b_ref = next(it), next(it)
    o_ref = next(it)
    if emit_state:
        sf_ref, sb_ref = next(it), next(it)
    sprev_scr, dec_scr, dmat_scr = next(it), next(it), next(it)

    c = RET_CHUNK
    n_chunks = seq_len // c
    kw, vw = 2 * r_dk, 2 * r_dv

    lg_v = _log_sigmoid(dv_ref[...])
    chunk_f = jnp.exp(lg_v[0:1] * float(c))
    chunk_b = jnp.exp(lg_v[1:2] * float(c))

    @pl.when(pl.program_id(1) == 0)
    def _():
        lg_k = _log_sigmoid(dk_ref[...])
        lg_c = _log_sigmoid(dc_ref[...])
        idx = lax.broadcasted_iota(jnp.int32, (c, 1), 0).astype(F32)
        dec_scr[0] = jnp.exp(lg_k[0:1] * (idx + 1.0))
        dec_scr[1] = jnp.exp(lg_k[1:2] * (c - idx))
        dec_scr[2] = jnp.exp(lg_k[0:1] * (c - 1.0 - idx))
        dec_scr[3] = jnp.exp(lg_k[1:2] * idx)
        diff = (lax.broadcasted_iota(jnp.int32, (c, c), 0)
                - lax.broadcasted_iota(jnp.int32, (c, c), 1)).astype(F32)
        for head in range(2):
            lf = lg_c[0:1, head * c:(head + 1) * c]
            lb = lg_c[1:2, head * c:(head + 1) * c]
            fwd = jnp.where(diff >= 0, jnp.exp(lf * jnp.maximum(diff, 0.0)), 0.0)
            bwd = jnp.where(diff <= 0, jnp.exp(lb * jnp.maximum(-diff, 0.0)), 0.0)
            dmat_scr[head] = fwd + bwd

    same_head = ((lax.broadcasted_iota(jnp.int32, (kw, vw), 0) // r_dk)
                 == (lax.broadcasted_iota(jnp.int32, (kw, vw), 1) // r_dv))
    k_lane = lax.broadcasted_iota(jnp.int32, (c, kw), 1)

    def load_state(ref):
        top = jnp.concatenate([ref[0, 0, 0], jnp.zeros((r_dk, r_dv), F32)], axis=1)
        bot = jnp.concatenate([jnp.zeros((r_dk, r_dv), F32), ref[0, 0, 1]], axis=1)
        return jnp.concatenate([top, bot], axis=0)

    def store_state(ref, s):
        ref[0, 0, 0] = s[:r_dk, :r_dv]
        ref[0, 0, 1] = s[r_dk:, r_dv:]

    def outer_kv(kd, v):
        kv = lax.dot_general(kd.astype(BF16), v, (((0,), (0,)), ((), ())), preferred_element_type=F32)
        return jnp.where(same_head, kv, 0.0)

    def rows(n):
        return pl.ds(pl.multiple_of(n * c, c), c)

    def fwd_body(n, s):
        sprev_scr[n] = s
        k = rk_ref[rows(n), :].astype(F32)
        return chunk_f * s + outer_kv(k * dec_scr[2], rv_ref[rows(n), :])

    s_f = load_state(s0f_ref) if has_init else jnp.zeros((kw, vw), F32)
    s_f = lax.fori_loop(0, n_chunks, fwd_body, s_f, unroll=min(8, n_chunks))
    if emit_state:
        store_state(sf_ref, s_f)

    def bwd_body(j, s_b):
        n = n_chunks - 1 - j
        q = rq_ref[rows(n), :]
        k = rk_ref[rows(n), :]
        v = rv_ref[rows(n), :]
        zero = jnp.zeros_like(q)
        q2 = jnp.concatenate([jnp.where(k_lane < r_dk, q, zero), jnp.where(k_lane >= r_dk, q, zero)], axis=0)
        sc = lax.dot_general(q2, k, (((1,), (1,)), ((), ())), preferred_element_type=F32)
        intra = jnp.concatenate(
            [jnp.dot((sc[h * c:(h + 1) * c] * dmat_scr[h]).astype(BF16), v[:, h * r_dv:(h + 1) * r_dv],
                     preferred_element_type=F32) for h in range(2)], axis=1)
        qf = q.astype(F32)
        q_both = jnp.concatenate([(qf * dec_scr[0]).astype(BF16), (qf * dec_scr[1]).astype(BF16)], axis=1)
        s_both = jnp.concatenate([sprev_scr[n], s_b], axis=0).astype(BF16)
        o = intra + jnp.dot(q_both, s_both, preferred_element_type=F32)
        o = jnp.concatenate([_rms(o[:, h * r_dv:(h + 1) * r_dv]) for h in range(2)], axis=1)
        o_ref[rows(n), :] = (rg_ref[rows(n), :].astype(F32) * o).astype(BF16)
        return chunk_b * s_b + outer_kv(k.astype(F32) * dec_scr[3], v)

    s_b = load_state(s0b_ref) if has_init else jnp.zeros((kw, vw), F32)
    s_b = lax.fori_loop(0, n_chunks, bwd_body, s_b, unroll=min(4, n_chunks))
    if emit_state:
        store_state(sb_ref, s_b)


def _retention(rq, rk, rv, rg, decays, init, layer, *, batch, seq_len, r_heads, r_dk, r_dv, emit_state):
    t_rows = rq.shape[0]
    pairs = r_heads // 2
    kw, vw = 2 * r_dk, 2 * r_dv
    has_init = init is not None
    in_specs = [pl.BlockSpec((seq_len, kw), lambda p, b: (b, p)),
                pl.BlockSpec((seq_len, kw), lambda p, b: (b, p)),
                pl.BlockSpec((seq_len, vw), lambda p, b: (b, p)),
                pl.BlockSpec((seq_len, vw), lambda p, b: (b, p)),
                pl.BlockSpec((2, kw), lambda p, b: (0, p)),
                pl.BlockSpec((2, vw), lambda p, b: (0, p)),
                pl.BlockSpec((2, 2 * RET_CHUNK), lambda p, b: (0, p))]
    args = [rq, rk, rv, rg, *decays]
    state_spec = pl.BlockSpec((1, 1, 2, r_dk, r_dv), lambda p, b: (b, layer if has_init else 0, p, 0, 0))
    if has_init:
        in_specs += [state_spec, state_spec]
        args += list(init)
    out_specs = [pl.BlockSpec((seq_len, vw), lambda p, b: (b, p))]
    out_shape = [jax.ShapeDtypeStruct((t_rows, r_heads * r_dv), BF16)]
    if emit_state:
        new_spec = pl.BlockSpec((1, 1, 2, r_dk, r_dv), lambda p, b: (b, 0, p, 0, 0))
        out_specs += [new_spec, new_spec]
        out_shape += [jax.ShapeDtypeStruct((batch, 1, r_heads, r_dk, r_dv), F32)] * 2
    return pl.pallas_call(
        functools.partial(_retention_kernel, r_dk=r_dk, r_dv=r_dv, seq_len=seq_len, has_init=has_init,
                          emit_state=emit_state),
        grid=(pairs, batch),
        in_specs=in_specs, out_specs=out_specs, out_shape=out_shape,
        scratch_shapes=[pltpu.VMEM((seq_len // RET_CHUNK, kw, vw), F32),
                        pltpu.VMEM((4, RET_CHUNK, kw), F32), pltpu.VMEM((2, RET_CHUNK, RET_CHUNK), F32)],
        compiler_params=_params(2),
        name="retention",
    )(*args)


def _merge_kernel(ya_ref, yr_ref, ga_ref, gr_ref, x_ref, g1_ref, sh_ref, sc_ref, n2_ref,
                  wa_ref, wr_ref, wo_ref, x1_ref, h2_ref):
    y = (ga_ref[...].astype(F32) * jnp.dot(ya_ref[...], wa_ref[...], preferred_element_type=F32)
         + gr_ref[...].astype(F32) * jnp.dot(yr_ref[...], wr_ref[...], preferred_element_type=F32))
    x1 = x_ref[...] + g1_ref[0] * jnp.dot(y.astype(BF16), wo_ref[...], preferred_element_type=F32)
    x1_ref[...] = x1
    h2 = _rms(x1) * n2_ref[...]
    h2_ref[...] = (h2 * (1.0 + sc_ref[0]) + sh_ref[0]).astype(BF16)


def _merge(ya, yr, ga, gr, x2d, mod3, mod_row, norm2_g, wa, wr, wo):
    t_rows, d = x2d.shape
    tm = ROW_TILE

    def row(i):
        return (i, 0)

    def mod(j):
        return pl.BlockSpec((1, 1, d), lambda i: (mod_row(i), 0, j))

    act = pl.BlockSpec((tm, d), row)
    return pl.pallas_call(
        _merge_kernel,
        grid=(t_rows // tm,),
        in_specs=[act, act, act, act, act, mod(2), mod(3), mod(4), _resident((1, d)),
                  _resident(wa.shape), _resident(wr.shape), _resident(wo.shape)],
        out_specs=[act, act],
        out_shape=[jax.ShapeDtypeStruct((t_rows, d), F32), jax.ShapeDtypeStruct((t_rows, d), BF16)],
        compiler_params=_params(1),
        name="merge",
    )(ya, yr, ga, gr, x2d, mod3, mod3, mod3, norm2_g.reshape(1, d), wa, wr, wo)


def _ffn_kernel(h_ref, hp_ref, hn_ref, x_ref, g2_ref, wu_ref, cw_ref, cb_ref, wd_ref, fg_ref,
                o_ref, u_scr, *, d_ff, seq_len, final_norm):
    tm = h_ref.shape[0]
    tile = pl.program_id(0) % (seq_len // tm)
    hp = jnp.where(tile == 0, jnp.zeros_like(hp_ref), hp_ref[...])
    hn = jnp.where(tile == seq_len // tm - 1, jnp.zeros_like(hn_ref), hn_ref[...])
    h_ext = jnp.concatenate([hp, h_ref[...], hn], axis=0)
    n_split = 2
    ck = d_ff // n_split
    acc = None
    for j in range(n_split):
        halves = []
        for base in (j * ck, d_ff + j * ck):
            u = jnp.dot(h_ext, wu_ref[:, base:base + ck], preferred_element_type=F32)
            rows = u.shape[0]
            prev = pltpu.roll(u, 1, 0)[HALO:HALO + tm]
            here = u[HALO:HALO + tm]
            nxt = pltpu.roll(u, rows - 1, 0)[HALO:HALO + tm]
            cols = slice(base, base + ck)
            halves.append(cw_ref[0:1, cols] * prev + cw_ref[1:2, cols] * here
                          + cw_ref[2:3, cols] * nxt + cb_ref[:, cols])
        a, g = halves
        act = (g * jax.nn.sigmoid(g) * a).astype(BF16)
        part = jnp.dot(act, wd_ref[j * ck:(j + 1) * ck, :], preferred_element_type=F32)
        acc = part if acc is None else acc + part
    x2 = x_ref[...] + g2_ref[0] * acc
    if final_norm:
        x2 = _rms(x2) * fg_ref[...]
    o_ref[...] = x2


def _conv_ffn(h2, x1, mod3, mod_row, w_up, conv_w, conv_b, w_down, final_g, *, seq_len, final_norm):
    t_rows, d = x1.shape
    d_ff = w_down.shape[0]
    tm = min(ROW_TILE, seq_len)
    assert seq_len % tm == 0
    per = tm // HALO
    last = t_rows // HALO - 1

    def row(i):
        return (i, 0)

    return pl.pallas_call(
        functools.partial(_ffn_kernel, d_ff=d_ff, seq_len=seq_len, final_norm=final_norm),
        grid=(t_rows // tm,),
        in_specs=[pl.BlockSpec((tm, d), row),
                  pl.BlockSpec((HALO, d), lambda i: (jnp.maximum(i * per - 1, 0), 0)),
                  pl.BlockSpec((HALO, d), lambda i: (jnp.minimum((i + 1) * per, last), 0)),
                  pl.BlockSpec((tm, d), row),
                  pl.BlockSpec((1, 1, d), lambda i: (mod_row(i), 0, 5)),
                  _resident(w_up.shape), _resident(conv_w.shape), _resident((1, 2 * d_ff)),
                  _resident(w_down.shape), _resident((1, d))],
        out_specs=pl.BlockSpec((tm, d), row),
        out_shape=jax.ShapeDtypeStruct((t_rows, d), F32),
        scratch_shapes=[pltpu.VMEM((tm + 2 * HALO, d_ff // 2), F32)],
        compiler_params=_params(1),
        name="conv_ffn",
    )(h2, h2, h2, x1, mod3, w_up, conv_w, conv_b.reshape(1, 2 * d_ff), w_down, final_g.reshape(1, d))


def kernel(x_prompt, x_sample, c, cache_k, cache_v, state_ret_fwd, state_ret_bwd, c_ctx, norm1_g, norm2_g, w_mod, b_mod, w_in, q_norm_g, k_norm_g, decay_fwd, decay_bwd, w_att_o, w_ret_o, w_out, w_up, conv_w, conv_b, w_down, final_g):
    batch_p, seq_p, d_model = x_prompt.shape
    batch_s, seq_s, _ = x_sample.shape
    depth = w_in.shape[0]
    n_kv, past, hd = cache_k.shape[2], cache_k.shape[3], cache_k.shape[4]
    r_heads, r_dk, r_dv = state_ret_fwd.shape[2], state_ret_fwd.shape[3], state_ret_fwd.shape[4]
    n_heads = w_att_o.shape[1] // hd
    group = n_heads // n_kv
    dims = (n_heads, n_kv, hd, r_heads, r_dk, r_dv, d_model)
    assert seq_s % GRID_W == 0 and seq_s % ROW_TILE == 0 and ROW_TILE % seq_p == 0
    assert hd == 128 and 2 * r_dk == 128 and r_heads % 2 == 0
    assert seq_p % RET_CHUNK == 0 and seq_s % RET_CHUNK == 0

    xp = x_prompt.reshape(batch_p * seq_p, d_model)
    xs = x_sample.reshape(batch_s * seq_s, d_model)
    cond = jnp.concatenate([c_ctx[None], c, jnp.zeros((8 - 1 - batch_s, d_model), F32)], axis=0)

    pos_p = np.tile(np.arange(seq_p), ROW_TILE // seq_p)
    ret_tabs_p = _rope_tables(pos_p, 128)
    ret_tabs_s = _rope_tables(past + np.arange(seq_s), 128)
    t = np.arange(seq_s)
    row_t, col_t = _rope_tables(t // GRID_W, 64), _rope_tables(t % GRID_W, 64)
    att_tabs_s = tuple(np.concatenate([a, b], axis=1) for a, b in zip(row_t, col_t))

    tiles_per_seq = seq_s // ROW_TILE

    def mod_row_p(i):
        return 0

    def mod_row_s(i):
        return 1 + i // tiles_per_seq

    ck_bf, cv_bf = cache_k.astype(BF16), cache_v.astype(BF16)
    new_k, new_v, new_sf, new_sb = [], [], [], []
    for l in range(depth):
        last = l == depth - 1
        mod3 = _modulation(cond, w_mod[l], b_mod[l]).reshape(8, 1, 6 * d_model)
        w_in_b, wa_b, wr_b, wo_b = (w[l].astype(BF16) for w in (w_in, w_att_o, w_ret_o, w_out))
        wu_b, wd_b = w_up[l].astype(BF16), w_down[l].astype(BF16)
        decays = tuple(jnp.stack([jnp.repeat(decay_fwd[l], r), jnp.repeat(decay_bwd[l], r)])
                       for r in (r_dk, r_dv, RET_CHUNK))

        def layer(x2d, mod_row, ret_tabs, att_tabs, ctx, init, batch, seq_len, is_prompt):
            outs = _in_proj(x2d, mod3, mod_row, norm1_g[l], w_in_b, q_norm_g[l], k_norm_g[l], ret_tabs,
                            att_tabs, dims=dims, seq_len=seq_len, emit_cache=is_prompt)
            q, k, v, rq, rk, rv, rg, ga, gr = outs[:9]
            ya = _attention(q, k, v, ctx, l, batch=batch, seq_len=seq_len, n_kv=n_kv, group=group, hd=hd)
            ret = _retention(rq, rk, rv, rg, decays, init, l, batch=batch, seq_len=seq_len,
                             r_heads=r_heads, r_dk=r_dk, r_dv=r_dv, emit_state=is_prompt)
            x1, h2 = _merge(ya, ret[0], ga, gr, x2d, mod3, mod_row, norm2_g[l], wa_b, wr_b, wo_b)
            x2 = _conv_ffn(h2, x1, mod3, mod_row, wu_b, conv_w[l], conv_b[l], wd_b, final_g,
                           seq_len=seq_len, final_norm=last)
            return x2, outs[9:], ret[1:]

        xp, cache_new, state_new = layer(xp, mod_row_p, ret_tabs_p, None, None, None, batch_p, seq_p, True)
        new_k.append(cache_new[0])
        new_v.append(cache_new[1])
        new_sf.append(state_new[0])
        new_sb.append(state_new[1])
        xs, _, _ = layer(xs, mod_row_s, ret_tabs_s, att_tabs_s, (ck_bf, cv_bf),
                         (state_ret_fwd, state_ret_bwd), batch_s, seq_s, False)

    def stack(parts):
        return parts[0] if depth == 1 else jnp.concatenate(parts, axis=1)

    return (xp.reshape(batch_p, seq_p, d_model), xs.reshape(batch_s, seq_s, d_model),
            stack(new_k), stack(new_v), stack(new_sf), stack(new_sb))
```

```python
import functools

import jax
import jax.numpy as jnp
import numpy as np
from jax import lax
from jax.experimental import pallas as pl
from jax.experimental.pallas import tpu as pltpu

GRID_W = 64
ROPE_THETA = 10000.0
EPS = 1e-6
RET_CHUNK = 256
ROW_TILE = 512
MERGE_TILE = 1024
Q_TILE = 512
KV_TILE = 2048
LOG2_E = float(np.log2(np.e))
SAFE_DENOMINATOR = 2.0 ** -80
HALO = 16
VMEM_LIMIT = 56 * 1024 * 1024

F32 = jnp.float32
BF16 = jnp.bfloat16


def _params(n_axes):
    return pltpu.CompilerParams(dimension_semantics=("arbitrary",) * n_axes,
                                vmem_limit_bytes=VMEM_LIMIT)


def _resident(shape):
    nd = len(shape)
    return pl.BlockSpec(shape, lambda *_: (0,) * nd, pipeline_mode=pl.Buffered(1))


def _rms(x):
    return x * lax.rsqrt(jnp.mean(x * x, axis=-1, keepdims=True) + EPS)


def _rotate(t, cos, sin_signed):
    lane = lax.broadcasted_iota(jnp.int32, t.shape, 1)
    partner = jnp.where(lane % 64 < 32, pltpu.roll(t, 96, 1), pltpu.roll(t, 32, 1))
    return t * cos + partner * sin_signed


def _rope_tables(pos, width):
    half = 32
    inv = np.float32(ROPE_THETA) ** (-np.arange(half, dtype=np.float32) / np.float32(half))
    ang = np.asarray(pos, np.float32)[:, None] * inv[None, :]
    cos, sin = np.cos(ang), np.sin(ang)
    reps = width // 64
    return (np.tile(np.concatenate([cos, cos], axis=1), (1, reps)),
            np.tile(np.concatenate([-sin, sin], axis=1), (1, reps)))


def _mod_kernel(cond_ref, w_ref, b_ref, o_ref):
    cnd = cond_ref[...]
    s = (cnd * jax.nn.sigmoid(cnd)).astype(BF16)
    o_ref[...] = jnp.dot(s, w_ref[...].astype(BF16), preferred_element_type=F32) + b_ref[...]


def _modulation(cond, w_mod, b_mod):
    rows, d = cond.shape
    n = w_mod.shape[1]
    tn = 1024
    return pl.pallas_call(
        _mod_kernel,
        grid=(n // tn,),
        in_specs=[pl.BlockSpec((rows, d), lambda j: (0, 0)),
                  pl.BlockSpec((d, tn), lambda j: (0, j)),
                  pl.BlockSpec((1, tn), lambda j: (0, j))],
        out_specs=pl.BlockSpec((rows, tn), lambda j: (0, j)),
        out_shape=jax.ShapeDtypeStruct((rows, n), F32),
        compiler_params=_params(1),
        name="modulation",
    )(cond, w_mod, b_mod.reshape(1, n))


def _in_proj_kernel(*refs, dims, rope_qk, emit_cache, seq_len):
    n_heads, n_kv, hd, r_heads, r_dk, r_dv, d_model = dims
    it = iter(refs)
    x_ref, sh_ref, sc_ref, g_ref, w_ref, qg_ref, kg_ref = (next(it) for _ in range(7))
    rcos_ref, rsin_ref = next(it), next(it)
    if rope_qk:
        acos_ref, asin_ref = next(it), next(it)
    q_ref, k_ref, v_ref, rq_ref, rk_ref, rv_ref, rg_ref, ga_ref, gr_ref = (next(it) for _ in range(9))
    if emit_cache:
        ck_ref, cv_ref = next(it), next(it)

    x = x_ref[...]
    h = _rms(x) * g_ref[...]
    h = (h * (1.0 + sc_ref[0]) + sh_ref[0]).astype(BF16)

    col = [0]

    def proj(width):
        lo = col[0]
        col[0] = lo + width
        return jnp.dot(h, w_ref[:, lo:lo + width], preferred_element_type=F32)

    tm = x.shape[0]
    zq = proj(n_heads * hd)
    for i in range(n_heads):
        t = _rms(zq[:, i * hd:(i + 1) * hd]) * qg_ref[...]
        if rope_qk:
            t = _rotate(t, acos_ref[...], asin_ref[...])
        q_ref[:, i * hd:(i + 1) * hd] = (t * (hd ** -0.5 * LOG2_E)).astype(BF16)
    zk = proj(n_kv * hd)
    zv = proj(n_kv * hd)
    v_ref[...] = zv.astype(BF16)
    for i in range(n_kv):
        t = _rms(zk[:, i * hd:(i + 1) * hd]) * kg_ref[...]
        if emit_cache:
            for s in range(tm // seq_len):
                ck_ref[s, 0, i] = t[s * seq_len:(s + 1) * seq_len]
                cv_ref[s, 0, i] = zv[s * seq_len:(s + 1) * seq_len, i * hd:(i + 1) * hd]
        if rope_qk:
            t = _rotate(t, acos_ref[...], asin_ref[...])
        k_ref[:, i * hd:(i + 1) * hd] = t.astype(BF16)
    zrq = proj(r_heads * r_dk)
    zrk = proj(r_heads * r_dk)
    for i in range(r_heads * r_dk // 128):
        sl = slice(i * 128, (i + 1) * 128)
        rq_ref[:, sl] = _rotate(zrq[:, sl], rcos_ref[...], rsin_ref[...]).astype(BF16)
        t = _rotate(zrk[:, sl], rcos_ref[...], rsin_ref[...])
        rk_ref[:, sl] = (t * (r_dk ** -0.5)).astype(BF16)
    rv_ref[...] = proj(r_heads * r_dv).astype(BF16)
    zg = proj(r_heads * r_dv)
    rg_ref[...] = (zg * jax.nn.sigmoid(zg)).astype(BF16)
    ga_ref[...] = jax.nn.sigmoid(proj(d_model)).astype(BF16)
    gr_ref[...] = jax.nn.sigmoid(proj(d_model)).astype(BF16)


def _in_proj(x2d, mod3, mod_row, norm_g, w_in, q_g, k_g, ret_tabs, att_tabs, *, dims, seq_len,
             emit_cache):
    n_heads, n_kv, hd, r_heads, r_dk, r_dv, d_model = dims
    t_rows = x2d.shape[0]
    tm = ROW_TILE
    rope_qk = att_tabs is not None
    tab_blocks = ret_tabs[0].shape[0] // tm

    def row(i):
        return (i, 0)

    def tab(i):
        return (i % tab_blocks, 0)

    in_specs = [pl.BlockSpec((tm, d_model), row),
                pl.BlockSpec((1, 1, d_model), lambda i: (mod_row(i, tm), 0, 0)),
                pl.BlockSpec((1, 1, d_model), lambda i: (mod_row(i, tm), 0, 1)),
                _resident((1, d_model)),
                _resident(w_in.shape),
                _resident((1, hd)), _resident((1, hd))]
    args = [x2d, mod3, mod3, norm_g.reshape(1, d_model), w_in, q_g.reshape(1, hd), k_g.reshape(1, hd)]
    for tb in ret_tabs:
        in_specs.append(pl.BlockSpec((tm, 128), tab))
        args.append(tb)
    if rope_qk:
        for tb in att_tabs:
            in_specs.append(pl.BlockSpec((tm, hd), tab))
            args.append(tb)
    widths = [n_heads * hd, n_kv * hd, n_kv * hd, r_heads * r_dk, r_heads * r_dk,
              r_heads * r_dv, r_heads * r_dv, d_model, d_model]
    out_specs = [pl.BlockSpec((tm, w), row) for w in widths]
    out_shape = [jax.ShapeDtypeStruct((t_rows, w), BF16) for w in widths]
    if emit_cache:
        nb = t_rows // seq_len
        for _ in range(2):
            out_specs.append(pl.BlockSpec((tm // seq_len, 1, n_kv, seq_len, hd),
                                          lambda i: (i, 0, 0, 0, 0)))
            out_shape.append(jax.ShapeDtypeStruct((nb, 1, n_kv, seq_len, hd), F32))
    return pl.pallas_call(
        functools.partial(_in_proj_kernel, dims=dims, rope_qk=rope_qk, emit_cache=emit_cache,
                          seq_len=seq_len),
        grid=(t_rows // tm,),
        in_specs=in_specs, out_specs=out_specs, out_shape=out_shape,
        compiler_params=_params(1),
        name="in_proj",
    )(*args)


def _attention_kernel(*refs, group, hd, seq_len, has_ctx):
    if has_ctx:
        q_ref, k_ref, v_ref, ck_ref, cv_ref, o_ref, vt_scr, k2_scr = refs
    else:
        q_ref, k_ref, v_ref, o_ref, vt_scr, k2_scr = refs
    tq = q_ref.shape[0]
    tk = min(KV_TILE, seq_len)
    chunks = [(k_ref, c * tk, c * tk, tk) for c in range(seq_len // tk)]
    if has_ctx:
        chunks.append((None, 0, seq_len, ck_ref.shape[3]))

    def keys(ref, row, width):
        return ck_ref[0, 0, 0] if ref is None else ref[row:row + width, :]

    @pl.when(pl.program_id(2) == 0)
    def _():
        k2 = None
        for ref, row, off, width in chunks:
            vals = cv_ref[0, 0, 0] if ref is None else v_ref[row:row + width, :]
            vt_scr[:hd, off:off + width] = vals.astype(F32).T.astype(BF16)
            kf = keys(ref, row, width).astype(F32)
            k2_c = jnp.max(jnp.sum(kf * kf, axis=1, keepdims=True), axis=0, keepdims=True)
            k2 = k2_c if k2 is None else jnp.maximum(k2, k2_c)
        vt_scr[hd:, :] = jnp.ones((vt_scr.shape[0] - hd, vt_scr.shape[1]), BF16)
        k2_scr[...] = jnp.broadcast_to(k2, k2_scr.shape)

    qt = jnp.concatenate([q_ref[:, g * hd:(g + 1) * hd].astype(F32).T.astype(BF16) for g in range(group)],
                         axis=1)

    def write(acc):
        out = acc[:hd] / acc[hd:hd + 1]
        for g in range(group):
            o_ref[:, g * hd:(g + 1) * hd] = out[:, g * tq:(g + 1) * tq].T.astype(BF16)

    qf = qt.astype(F32)
    bound = jnp.sqrt(jnp.sum(qf * qf, axis=0, keepdims=True) * k2_scr[0:1, 0:1])
    acc = None
    for ref, row, off, width in chunks:
        st = jnp.dot(keys(ref, row, width), qt, preferred_element_type=F32)
        p = jnp.exp2(st - bound).astype(BF16)
        pv = jnp.dot(vt_scr[:, off:off + width], p, preferred_element_type=F32)
        acc = pv if acc is None else acc + pv
    safe = jnp.min(acc[hd:hd + 1]) >= SAFE_DENOMINATOR

    @pl.when(safe)
    def _():
        write(acc)

    @pl.when(jnp.logical_not(safe))
    def _():
        m = acc_x = None
        for ref, row, off, width in chunks:
            st = jnp.dot(keys(ref, row, width), qt, preferred_element_type=F32)
            m_c = jnp.max(st, axis=0, keepdims=True)
            m_new = m_c if m is None else jnp.maximum(m, m_c)
            p = jnp.exp2(st - m_new).astype(BF16)
            pv = jnp.dot(vt_scr[:, off:off + width], p, preferred_element_type=F32)
            acc_x = pv if m is None else jnp.exp2(m - m_new) * acc_x + pv
            m = m_new
        write(acc_x)


def _attention(q, k, v, ctx, layer, *, batch, seq_len, n_kv, group, hd):
    t_rows = q.shape[0]
    tq = min(Q_TILE, seq_len)
    nq = seq_len // tq
    has_ctx = ctx is not None
    in_specs = [pl.BlockSpec((tq, group * hd), lambda b, h, i: (b * nq + i, h)),
                pl.BlockSpec((seq_len, hd), lambda b, h, i: (b, h)),
                pl.BlockSpec((seq_len, hd), lambda b, h, i: (b, h))]
    args = [q, k, v]
    n_keys = seq_len
    if has_ctx:
        past = ctx[0].shape[3]
        n_keys += past
        for a in ctx:
            in_specs.append(pl.BlockSpec((1, 1, 1, past, hd), lambda b, h, i: (b, layer, h, 0, 0)))
            args.append(a)
    return pl.pallas_call(
        functools.partial(_attention_kernel, group=group, hd=hd, seq_len=seq_len, has_ctx=has_ctx),
        grid=(batch, n_kv, nq),
        in_specs=in_specs,
        out_specs=pl.BlockSpec((tq, group * hd), lambda b, h, i: (b * nq + i, h)),
        out_shape=jax.ShapeDtypeStruct((t_rows, n_kv * group * hd), BF16),
        scratch_shapes=[pltpu.VMEM((hd + HALO, n_keys), BF16), pltpu.VMEM((8, 128), F32)],
        compiler_params=_params(3),
        name="attention",
    )(*args)


def _log_sigmoid(x):
    return jnp.minimum(x, 0.0) - jnp.log1p(jnp.exp(-jnp.abs(x)))


def _retention_kernel(*refs, r_dk, r_dv, seq_len, has_init, emit_state):
    it = iter(refs)
    rq_ref, rk_ref, rv_ref, rg_ref, dk_ref, dv_ref, dc_ref = (next(it) for _ in range(7))
    if has_init:
        s0f_ref, s0b_ref = next(it), next(it)
    o_ref = next(it)
    if emit_state:
        sf_ref, sb_ref = next(it), next(it)
    sprev_scr, dec_scr, dmat_scr = next(it), next(it), next(it)

    c = RET_CHUNK
    n_chunks = seq_len // c
    kw, vw = 2 * r_dk, 2 * r_dv

    lg_v = _log_sigmoid(dv_ref[...])
    chunk_f = jnp.exp(lg_v[0:1] * float(c))
    chunk_b = jnp.exp(lg_v[1:2] * float(c))

    @pl.when(pl.program_id(1) == 0)
    def _():
        lg_k = _log_sigmoid(dk_ref[...])
        lg_c = _log_sigmoid(dc_ref[...])
        idx = lax.broadcasted_iota(jnp.int32, (c, 1), 0).astype(F32)
        dec_scr[0] = jnp.exp(lg_k[0:1] * (idx + 1.0))
        dec_scr[1] = jnp.exp(lg_k[1:2] * (c - idx))
        dec_scr[2] = jnp.exp(lg_k[0:1] * (c - 1.0 - idx))
        dec_scr[3] = jnp.exp(lg_k[1:2] * idx)
        diff = (lax.broadcasted_iota(jnp.int32, (c, c), 0)
                - lax.broadcasted_iota(jnp.int32, (c, c), 1)).astype(F32)
        for head in range(2):
            lf = lg_c[0:1, head * c:(head + 1) * c]
            lb = lg_c[1:2, head * c:(head + 1) * c]
            fwd = jnp.where(diff >= 0, jnp.exp(lf * jnp.maximum(diff, 0.0)), 0.0)
            bwd = jnp.where(diff <= 0, jnp.exp(lb * jnp.maximum(-diff, 0.0)), 0.0)
            dmat_scr[head] = fwd + bwd

    same_head = ((lax.broadcasted_iota(jnp.int32, (kw, vw), 0) // r_dk)
                 == (lax.broadcasted_iota(jnp.int32, (kw, vw), 1) // r_dv))
    k_lane = lax.broadcasted_iota(jnp.int32, (c, kw), 1)

    def load_state(ref):
        top = jnp.concatenate([ref[0, 0, 0], jnp.zeros((r_dk, r_dv), F32)], axis=1)
        bot = jnp.concatenate([jnp.zeros((r_dk, r_dv), F32), ref[0, 0, 1]], axis=1)
        return jnp.concatenate([top, bot], axis=0)

    def store_state(ref, s):
        ref[0, 0, 0] = s[:r_dk, :r_dv]
        ref[0, 0, 1] = s[r_dk:, r_dv:]

    def outer_kv(kd, v):
        kv = lax.dot_general(kd.astype(BF16), v, (((0,), (0,)), ((), ())), preferred_element_type=F32)
        return jnp.where(same_head, kv, 0.0)

    def rows(n):
        return pl.ds(pl.multiple_of(n * c, c), c)

    def fwd_body(n, s):
        sprev_scr[n] = s
        k = rk_ref[rows(n), :].astype(F32)
        return chunk_f * s + outer_kv(k * dec_scr[2], rv_ref[rows(n), :])

    s_f = load_state(s0f_ref) if has_init else jnp.zeros((kw, vw), F32)
    s_f = lax.fori_loop(0, n_chunks, fwd_body, s_f, unroll=min(8, n_chunks))
    if emit_state:
        store_state(sf_ref, s_f)

    def bwd_body(j, s_b):
        n = n_chunks - 1 - j
        q = rq_ref[rows(n), :]
        k = rk_ref[rows(n), :]
        v = rv_ref[rows(n), :]
        zero = jnp.zeros_like(q)
        q2 = jnp.concatenate([jnp.where(k_lane < r_dk, q, zero), jnp.where(k_lane >= r_dk, q, zero)], axis=0)
        sc = lax.dot_general(q2, k, (((1,), (1,)), ((), ())), preferred_element_type=F32)
        intra = jnp.concatenate(
            [jnp.dot((sc[h * c:(h + 1) * c] * dmat_scr[h]).astype(BF16), v[:, h * r_dv:(h + 1) * r_dv],
                     preferred_element_type=F32) for h in range(2)], axis=1)
        qf = q.astype(F32)
        q_both = jnp.concatenate([(qf * dec_scr[0]).astype(BF16), (qf * dec_scr[1]).astype(BF16)], axis=1)
        s_both = jnp.concatenate([sprev_scr[n], s_b], axis=0).astype(BF16)
        o = intra + jnp.dot(q_both, s_both, preferred_element_type=F32)
        o = jnp.concatenate([_rms(o[:, h * r_dv:(h + 1) * r_dv]) for h in range(2)], axis=1)
        o_ref[rows(n), :] = (rg_ref[rows(n), :].astype(F32) * o).astype(BF16)
        return chunk_b * s_b + outer_kv(k.astype(F32) * dec_scr[3], v)

    s_b = load_state(s0b_ref) if has_init else jnp.zeros((kw, vw), F32)
    s_b = lax.fori_loop(0, n_chunks, bwd_body, s_b, unroll=min(4, n_chunks))
    if emit_state:
        store_state(sb_ref, s_b)


def _retention(rq, rk, rv, rg, decays, init, layer, *, batch, seq_len, r_heads, r_dk, r_dv, emit_state):
    t_rows = rq.shape[0]
    pairs = r_heads // 2
    kw, vw = 2 * r_dk, 2 * r_dv
    has_init = init is not None
    in_specs = [pl.BlockSpec((seq_len, kw), lambda p, b: (b, p)),
                pl.BlockSpec((seq_len, kw), lambda p, b: (b, p)),
                pl.BlockSpec((seq_len, vw), lambda p, b: (b, p)),
                pl.BlockSpec((seq_len, vw), lambda p, b: (b, p)),
                pl.BlockSpec((2, kw), lambda p, b: (0, p)),
                pl.BlockSpec((2, vw), lambda p, b: (0, p)),
                pl.BlockSpec((2, 2 * RET_CHUNK), lambda p, b: (0, p))]
    args = [rq, rk, rv, rg, *decays]
    state_spec = pl.BlockSpec((1, 1, 2, r_dk, r_dv), lambda p, b: (b, layer if has_init else 0, p, 0, 0))
    if has_init:
        in_specs += [state_spec, state_spec]
        args += list(init)
    out_specs = [pl.BlockSpec((seq_len, vw), lambda p, b: (b, p))]
    out_shape = [jax.ShapeDtypeStruct((t_rows, r_heads * r_dv), BF16)]
    if emit_state:
        new_spec = pl.BlockSpec((1, 1, 2, r_dk, r_dv), lambda p, b: (b, 0, p, 0, 0))
        out_specs += [new_spec, new_spec]
        out_shape += [jax.ShapeDtypeStruct((batch, 1, r_heads, r_dk, r_dv), F32)] * 2
    return pl.pallas_call(
        functools.partial(_retention_kernel, r_dk=r_dk, r_dv=r_dv, seq_len=seq_len, has_init=has_init,
                          emit_state=emit_state),
        grid=(pairs, batch),
        in_specs=in_specs, out_specs=out_specs, out_shape=out_shape,
        scratch_shapes=[pltpu.VMEM((seq_len // RET_CHUNK, kw, vw), F32),
                        pltpu.VMEM((4, RET_CHUNK, kw), F32), pltpu.VMEM((2, RET_CHUNK, RET_CHUNK), F32)],
        compiler_params=_params(2),
        name="retention",
    )(*args)


def _merge_kernel(ya_ref, yr_ref, ga_ref, gr_ref, x_ref, g1_ref, sh_ref, sc_ref, n2_ref,
                  wa_ref, wr_ref, wo_ref, x1_ref, h2_ref):
    y = (ga_ref[...].astype(F32) * jnp.dot(ya_ref[...], wa_ref[...], preferred_element_type=F32)
         + gr_ref[...].astype(F32) * jnp.dot(yr_ref[...], wr_ref[...], preferred_element_type=F32))
    x1 = x_ref[...] + g1_ref[0] * jnp.dot(y.astype(BF16), wo_ref[...], preferred_element_type=F32)
    x1_ref[...] = x1
    h2 = _rms(x1) * n2_ref[...]
    h2_ref[...] = (h2 * (1.0 + sc_ref[0]) + sh_ref[0]).astype(BF16)


def _merge(ya, yr, ga, gr, x2d, mod3, mod_row, norm2_g, wa, wr, wo):
    t_rows, d = x2d.shape
    tm = MERGE_TILE

    def row(i):
        return (i, 0)

    def mod(j):
        return pl.BlockSpec((1, 1, d), lambda i: (mod_row(i, tm), 0, j))

    act = pl.BlockSpec((tm, d), row)
    return pl.pallas_call(
        _merge_kernel,
        grid=(t_rows // tm,),
        in_specs=[act, act, act, act, act, mod(2), mod(3), mod(4), _resident((1, d)),
                  _resident(wa.shape), _resident(wr.shape), _resident(wo.shape)],
        out_specs=[act, act],
        out_shape=[jax.ShapeDtypeStruct((t_rows, d), F32), jax.ShapeDtypeStruct((t_rows, d), BF16)],
        compiler_params=_params(1),
        name="merge",
    )(ya, yr, ga, gr, x2d, mod3, mod3, mod3, norm2_g.reshape(1, d), wa, wr, wo)


def _ffn_kernel(h_ref, hp_ref, hn_ref, x_ref, g2_ref, wu_ref, cw_ref, cb_ref, wd_ref, fg_ref,
                o_ref, u_scr, *, d_ff, seq_len, final_norm):
    tm = h_ref.shape[0]
    tile = pl.program_id(0) % (seq_len // tm)
    hp = jnp.where(tile == 0, jnp.zeros_like(hp_ref), hp_ref[...])
    hn = jnp.where(tile == seq_len // tm - 1, jnp.zeros_like(hn_ref), hn_ref[...])
    h_ext = jnp.concatenate([hp, h_ref[...], hn], axis=0)
    n_split = 2
    ck = d_ff // n_split
    acc = None
    for j in range(n_split):
        halves = []
        for base in (j * ck, d_ff + j * ck):
            u = jnp.dot(h_ext, wu_ref[:, base:base + ck], preferred_element_type=F32)
            rows = u.shape[0]
            prev = pltpu.roll(u, 1, 0)[HALO:HALO + tm]
            here = u[HALO:HALO + tm]
            nxt = pltpu.roll(u, rows - 1, 0)[HALO:HALO + tm]
            cols = slice(base, base + ck)
            halves.append(cw_ref[0:1, cols] * prev + cw_ref[1:2, cols] * here
                          + cw_ref[2:3, cols] * nxt + cb_ref[:, cols])
        a, g = halves
        act = (g * jax.nn.sigmoid(g) * a).astype(BF16)
        part = jnp.dot(act, wd_ref[j * ck:(j + 1) * ck, :], preferred_element_type=F32)
        acc = part if acc is None else acc + part
    x2 = x_ref[...] + g2_ref[0] * acc
    if final_norm:
        x2 = _rms(x2) * fg_ref[...]
    o_ref[...] = x2


def _conv_ffn(h2, x1, mod3, mod_row, w_up, conv_w, conv_b, w_down, final_g, *, seq_len, final_norm):
    t_rows, d = x1.shape
    d_ff = w_down.shape[0]
    tm = min(ROW_TILE, seq_len)
    assert seq_len % tm == 0
    per = tm // HALO
    last = t_rows // HALO - 1

    def row(i):
        return (i, 0)

    return pl.pallas_call(
        functools.partial(_ffn_kernel, d_ff=d_ff, seq_len=seq_len, final_norm=final_norm),
        grid=(t_rows // tm,),
        in_specs=[pl.BlockSpec((tm, d), row),
                  pl.BlockSpec((HALO, d), lambda i: (jnp.maximum(i * per - 1, 0), 0)),
                  pl.BlockSpec((HALO, d), lambda i: (jnp.minimum((i + 1) * per, last), 0)),
                  pl.BlockSpec((tm, d), row),
                  pl.BlockSpec((1, 1, d), lambda i: (mod_row(i, tm), 0, 5)),
                  _resident(w_up.shape), _resident(conv_w.shape), _resident((1, 2 * d_ff)),
                  _resident(w_down.shape), _resident((1, d))],
        out_specs=pl.BlockSpec((tm, d), row),
        out_shape=jax.ShapeDtypeStruct((t_rows, d), F32),
        scratch_shapes=[pltpu.VMEM((tm + 2 * HALO, d_ff // 2), F32)],
        compiler_params=_params(1),
        name="conv_ffn",
    )(h2, h2, h2, x1, mod3, w_up, conv_w, conv_b.reshape(1, 2 * d_ff), w_down, final_g.reshape(1, d))


def kernel(x_prompt, x_sample, c, cache_k, cache_v, state_ret_fwd, state_ret_bwd, c_ctx, norm1_g, norm2_g, w_mod, b_mod, w_in, q_norm_g, k_norm_g, decay_fwd, decay_bwd, w_att_o, w_ret_o, w_out, w_up, conv_w, conv_b, w_down, final_g):
    batch_p, seq_p, d_model = x_prompt.shape
    batch_s, seq_s, _ = x_sample.shape
    depth = w_in.shape[0]
    n_kv, past, hd = cache_k.shape[2], cache_k.shape[3], cache_k.shape[4]
    r_heads, r_dk, r_dv = state_ret_fwd.shape[2], state_ret_fwd.shape[3], state_ret_fwd.shape[4]
    n_heads = w_att_o.shape[1] // hd
    group = n_heads // n_kv
    dims = (n_heads, n_kv, hd, r_heads, r_dk, r_dv, d_model)
    assert seq_s % GRID_W == 0 and seq_s % MERGE_TILE == 0 and ROW_TILE % seq_p == 0
    assert hd == 128 and 2 * r_dk == 128 and r_heads % 2 == 0
    assert seq_p % RET_CHUNK == 0 and seq_s % RET_CHUNK == 0

    xp = x_prompt.reshape(batch_p * seq_p, d_model)
    xs = x_sample.reshape(batch_s * seq_s, d_model)
    cond = jnp.concatenate([c_ctx[None], c, jnp.zeros((8 - 1 - batch_s, d_model), F32)], axis=0)

    pos_p = np.tile(np.arange(seq_p), ROW_TILE // seq_p)
    ret_tabs_p = _rope_tables(pos_p, 128)
    ret_tabs_s = _rope_tables(past + np.arange(seq_s), 128)
    t = np.arange(seq_s)
    row_t, col_t = _rope_tables(t // GRID_W, 64), _rope_tables(t % GRID_W, 64)
    att_tabs_s = tuple(np.concatenate([a, b], axis=1) for a, b in zip(row_t, col_t))

    def mod_row_p(i, tm):
        return 0

    def mod_row_s(i, tm):
        return 1 + (i * tm) // seq_s

    ck_bf, cv_bf = cache_k.astype(BF16), cache_v.astype(BF16)
    new_k, new_v, new_sf, new_sb = [], [], [], []
    for l in range(depth):
        last = l == depth - 1
        mod3 = _modulation(cond, w_mod[l], b_mod[l]).reshape(8, 1, 6 * d_model)
        w_in_b, wa_b, wr_b, wo_b = (w[l].astype(BF16) for w in (w_in, w_att_o, w_ret_o, w_out))
        wu_b, wd_b = w_up[l].astype(BF16), w_down[l].astype(BF16)
        decays = tuple(jnp.stack([jnp.repeat(decay_fwd[l], r), jnp.repeat(decay_bwd[l], r)])
                       for r in (r_dk, r_dv, RET_CHUNK))

        def layer(x2d, mod_row, ret_tabs, att_tabs, ctx, init, batch, seq_len, is_prompt):
            outs = _in_proj(x2d, mod3, mod_row, norm1_g[l], w_in_b, q_norm_g[l], k_norm_g[l], ret_tabs,
                            att_tabs, dims=dims, seq_len=seq_len, emit_cache=is_prompt)
            q, k, v, rq, rk, rv, rg, ga, gr = outs[:9]
            ya = _attention(q, k, v, ctx, l, batch=batch, seq_len=seq_len, n_kv=n_kv, group=group, hd=hd)
            ret = _retention(rq, rk, rv, rg, decays, init, l, batch=batch, seq_len=seq_len,
                             r_heads=r_heads, r_dk=r_dk, r_dv=r_dv, emit_state=is_prompt)
            x1, h2 = _merge(ya, ret[0], ga, gr, x2d, mod3, mod_row, norm2_g[l], wa_b, wr_b, wo_b)
            x2 = _conv_ffn(h2, x1, mod3, mod_row, wu_b, conv_w[l], conv_b[l], wd_b, final_g,
                           seq_len=seq_len, final_norm=last)
            return x2, outs[9:], ret[1:]

        xp, cache_new, state_new = layer(xp, mod_row_p, ret_tabs_p, None, None, None, batch_p, seq_p, True)
        new_k.append(cache_new[0])
        new_v.append(cache_new[1])
        new_sf.append(state_new[0])
        new_sb.append(state_new[1])
        xs, _, _ = layer(xs, mod_row_s, ret_tabs_s, att_tabs_s, (ck_bf, cv_bf),
                         (state_ret_fwd, state_ret_bwd), batch_s, seq_s, False)

    def stack(parts):
        return parts[0] if depth == 1 else jnp.concatenate(parts, axis=1)

    return (xp.reshape(batch_p, seq_p, d_model), xs.reshape(batch_s, seq_s, d_model),
            stack(new_k), stack(new_v), stack(new_sf), stack(new_sb))
```
